```python
import math
import jax, jax.numpy as jnp
from jax import lax
import numpy as np

D_MODEL = 1024
BATCH = 2
SEQ = 8192
DEPTH = 1
DEC_BATCH = 32
DEC_SEQ = 1
PAST_LEN = 8192
PAGE_SIZE = 128

MIX_WIDTH = D_MODEL
ATTN_WIDTH = MIX_WIDTH // 2
POOL_WIDTH = MIX_WIDTH - ATTN_WIDTH
HEAD_DIM = 64
N_HEADS = ATTN_WIDTH // HEAD_DIM
SB_SCALE = 1.0 / math.sqrt(HEAD_DIM)
SB_BIAS_INIT = -6.0
Q_BLOCK = 128
POOL_WINDOWS = (2, 4, 8, 16)
N_POOL_GROUPS = len(POOL_WINDOWS)
POOL_GROUP_WIDTH = POOL_WIDTH // N_POOL_GROUPS
POOL_STATE = max(POOL_WINDOWS) - 1
N_EXPERT_GROUPS = 4
EXPERTS_PER_GROUP = 8
N_EXPERTS = N_EXPERT_GROUPS * EXPERTS_PER_GROUP
TOP_K = 2
EXPERT_HIDDEN = D_MODEL // 2
MOE_BLOCK = 128
PLE_DIM = 256
RMS_EPS = 1e-6

kernel_name = 'hymba_stickbreak_pool_hmoe_step'


def rmsnorm(x, g):
    xf = x.astype(jnp.float32)
    xf = xf * lax.rsqrt(jnp.mean(xf * xf, axis=-1, keepdims=True) + RMS_EPS)
    return (xf * g.astype(jnp.float32)).astype(x.dtype)


def project(x, g_mix, w_in, g_q, g_k):
    b, t, _ = x.shape
    z = rmsnorm(x, g_mix) @ w_in
    q = z[..., :ATTN_WIDTH].reshape(b, t, N_HEADS, HEAD_DIM)
    k = z[..., ATTN_WIDTH:2 * ATTN_WIDTH].reshape(b, t, N_HEADS, HEAD_DIM)
    v = z[..., 2 * ATTN_WIDTH:3 * ATTN_WIDTH].reshape(b, t, N_HEADS, HEAD_DIM)
    u = z[..., 3 * ATTN_WIDTH:]
    return rmsnorm(q, g_q), rmsnorm(k, g_k), v, u


def stick_breaking(q, k, v, q_pos, k_pos, bias):
    z = jnp.einsum('bqhd,bkhd->bhqk', q.astype(jnp.float32), k.astype(jnp.float32)) * SB_SCALE
    z = z + bias.astype(jnp.float32)[None, :, None, None]
    mask = k_pos[None, :] < q_pos[:, None]
    log_beta = jax.nn.log_sigmoid(z)
    log_keep = jnp.where(mask, jax.nn.log_sigmoid(-z), 0.0)
    later = lax.cumsum(log_keep, axis=3, reverse=True) - log_keep
    a = jnp.where(mask, jnp.exp(log_beta + later), 0.0)
    return jnp.einsum('bhqk,bkhd->bqhd', a, v.astype(jnp.float32)).astype(v.dtype)


def stick_breaking_prompt(q, k, v, bias):
    b, s, h, d = q.shape
    n_blk = s // Q_BLOCK
    k_pos = jnp.arange(s)

    def one_block(i):
        qb = lax.dynamic_slice_in_dim(q, i * Q_BLOCK, Q_BLOCK, axis=1)
        q_pos = i * Q_BLOCK + jnp.arange(Q_BLOCK)
        return stick_breaking(qb, k, v, q_pos, k_pos, bias)

    out = lax.map(one_block, jnp.arange(n_blk))
    return out.transpose(1, 0, 2, 3, 4).reshape(b, s, h, d)


def pool_mix(u_ext, n_new, start_pos, w_pool, s_pool):
    b, length, c = u_ext.shape
    p0 = length - n_new
    uf = u_ext.astype(jnp.float32)
    cs = jnp.concatenate([jnp.zeros((b, 1, c), jnp.float32), jnp.cumsum(uf, axis=1)], axis=1)
    rows = np.arange(p0, length)
    pos = start_pos + np.arange(n_new)
    end = cs[:, rows + 1]
    u_new = uf[:, p0:]
    outs = []
    for gi, w in enumerate(POOL_WINDOWS):
        sl = slice(gi * POOL_GROUP_WIDTH, (gi + 1) * POOL_GROUP_WIDTH)
        beg = cs[:, np.maximum(rows + 1 - w, 0), sl]
        cnt = jnp.asarray(np.minimum(pos + 1, w), jnp.float32)[None, :, None]
        outs.append((end[..., sl] - beg) / cnt - u_new[..., sl])
    pooled = jnp.stack(outs, axis=2)
    y = jnp.einsum('btgc,gcd->btgd', pooled, w_pool.astype(jnp.float32)).reshape(b, n_new, c)
    return (y * s_pool.astype(jnp.float32)).astype(u_ext.dtype)


def hier_moe(x, w_rg, b_rg, w_re, b_re, w_gate, w_up, w_down):
    n = x.shape[0]
    p_grp = jax.nn.softmax((x @ w_rg).astype(jnp.float32) + b_rg.astype(jnp.float32), axis=-1)
    g_w, g_id = lax.top_k(p_grp, 1)
    le = ((x @ w_re).astype(jnp.float32) + b_re.astype(jnp.float32)).reshape(n, N_EXPERT_GROUPS, EXPERTS_PER_GROUP)
    le = jnp.take_along_axis(le, g_id[:, :, None], axis=1)[:, 0]
    e_w, e_id = lax.top_k(jax.nn.softmax(le, axis=-1), TOP_K)
    gates = g_w * e_w / jnp.sum(e_w, axis=-1, keepdims=True)
    expert = (g_id * EXPERTS_PER_GROUP + e_id).reshape(-1).astype(jnp.int32)
    n_assign = n * TOP_K
    tok = jnp.repeat(jnp.arange(n, dtype=jnp.int32), TOP_K)
    gate_flat = gates.reshape(-1)
    order = jnp.argsort(expert)
    e_s, tok_s, g_s = expert[order], tok[order], gate_flat[order]
    counts = jnp.bincount(expert, length=N_EXPERTS)
    starts = jnp.cumsum(counts) - counts
    pcounts = (counts + MOE_BLOCK - 1) // MOE_BLOCK * MOE_BLOCK
    pends = jnp.cumsum(pcounts)
    pstarts = pends - pcounts
    dest = pstarts[e_s] + jnp.arange(n_assign) - starts[e_s]
    n_blk = (n_assign + N_EXPERTS * (MOE_BLOCK - 1) + MOE_BLOCK - 1) // MOE_BLOCK
    n_slots = n_blk * MOE_BLOCK
    slot_tok = jnp.full((n_slots,), n, jnp.int32).at[dest].set(tok_s)
    slot_gate = jnp.zeros((n_slots,), jnp.float32).at[dest].set(g_s)
    blk_e = jnp.minimum(jnp.searchsorted(pends, jnp.arange(n_blk) * MOE_BLOCK, side='right'), N_EXPERTS - 1)
    x_pad = jnp.concatenate([x, jnp.zeros((1, x.shape[1]), x.dtype)], axis=0)

    def run_block(args):
        tok_b, e = args
        xb = x_pad[tok_b]
        hid = jax.nn.silu(xb @ w_gate[e]) * (xb @ w_up[e])
        return hid @ w_down[e]

    y_slots = lax.map(run_block, (slot_tok.reshape(n_blk, MOE_BLOCK), blk_e)).reshape(n_slots, -1)
    y = jnp.zeros((n + 1, x.shape[1]), jnp.float32).at[slot_tok].add(y_slots.astype(jnp.float32) * slot_gate[:, None])
    return y[:n].astype(x.dtype)


def finish(h, attn, pool, p, w_out, g_ffn, w_rg, b_rg, w_re, b_re, w_gate, w_up, w_down, g_ple, w_ple_gate, w_ple_proj):
    b, t, _ = h.shape
    mixed = jnp.concatenate([attn.reshape(b, t, ATTN_WIDTH), pool.astype(attn.dtype)], axis=-1)
    h = h + mixed @ w_out
    m = rmsnorm(h, g_ffn).reshape(b * t, D_MODEL)
    h = h + hier_moe(m, w_rg, b_rg, w_re, b_re, w_gate, w_up, w_down).reshape(b, t, D_MODEL)
    gate = jax.nn.sigmoid((rmsnorm(h, g_ple) @ w_ple_gate).astype(jnp.float32))
    h = h + (gate * (p @ w_ple_proj).astype(jnp.float32)).astype(h.dtype)
    return h


def setup_inputs(seed: int = 0) -> dict:
    key = jax.random.key(seed)
    ks = jax.random.split(key, 32)
    f32 = jnp.float32

    def nrm(k, shape, scale=1.0):
        return jax.random.normal(k, shape, f32) * scale

    n_pages = PAST_LEN // PAGE_SIZE
    n_used = DEC_BATCH * n_pages
    n_phys = n_used + max(1, n_used // 4)
    page_table = jax.random.permutation(ks[0], n_phys)[:n_used].reshape(DEC_BATCH, n_pages).astype(jnp.int32)
    n_in = 3 * ATTN_WIDTH + POOL_WIDTH
    return {
        'x_prompt': nrm(ks[1], (BATCH, SEQ, D_MODEL)),
        'x_sample': nrm(ks[2], (DEC_BATCH, DEC_SEQ, D_MODEL)),
        'cache_k': nrm(ks[3], (DEPTH, n_phys, PAGE_SIZE, N_HEADS, HEAD_DIM)),
        'cache_v': nrm(ks[4], (DEPTH, n_phys, PAGE_SIZE, N_HEADS, HEAD_DIM)),
        'state_pool': nrm(ks[5], (DEPTH, DEC_BATCH, POOL_STATE, POOL_WIDTH)),
        'page_table': page_table,
        'p_prompt': nrm(ks[6], (DEPTH, BATCH, SEQ, PLE_DIM)),
        'p_sample': nrm(ks[7], (DEPTH, DEC_BATCH, DEC_SEQ, PLE_DIM)),
        'g_mix': 1.0 + nrm(ks[8], (DEPTH, D_MODEL), 0.05),
        'w_in': nrm(ks[9], (DEPTH, D_MODEL, n_in), D_MODEL ** -0.5),
        'g_q': 1.0 + nrm(ks[10], (DEPTH, HEAD_DIM), 0.05),
        'g_k': 1.0 + nrm(ks[11], (DEPTH, HEAD_DIM), 0.05),
        'b_sb': SB_BIAS_INIT + nrm(ks[26], (DEPTH, N_HEADS), 0.5),
        'w_pool': nrm(ks[12], (DEPTH, N_POOL_GROUPS, POOL_GROUP_WIDTH, POOL_GROUP_WIDTH), POOL_GROUP_WIDTH ** -0.5),
        's_pool': 1.0 + nrm(ks[13], (DEPTH, POOL_WIDTH), 0.1),
        'w_out': nrm(ks[14], (DEPTH, MIX_WIDTH, D_MODEL), MIX_WIDTH ** -0.5),
        'g_ffn': 1.0 + nrm(ks[15], (DEPTH, D_MODEL), 0.05),
        'w_router_group': nrm(ks[16], (DEPTH, D_MODEL, N_EXPERT_GROUPS), D_MODEL ** -0.5),
        'b_router_group': nrm(ks[17], (DEPTH, N_EXPERT_GROUPS), 0.01),
        'w_router_expert': nrm(ks[18], (DEPTH, D_MODEL, N_EXPERTS), D_MODEL ** -0.5),
        'b_router_expert': nrm(ks[19], (DEPTH, N_EXPERTS), 0.01),
        'w_gate': nrm(ks[20], (DEPTH, N_EXPERTS, D_MODEL, EXPERT_HIDDEN), D_MODEL ** -0.5),
        'w_up': nrm(ks[21], (DEPTH, N_EXPERTS, D_MODEL, EXPERT_HIDDEN), D_MODEL ** -0.5),
        'w_down': nrm(ks[22], (DEPTH, N_EXPERTS, EXPERT_HIDDEN, D_MODEL), EXPERT_HIDDEN ** -0.5),
        'g_ple': 1.0 + nrm(ks[23], (DEPTH, D_MODEL), 0.05),
        'w_ple_gate': nrm(ks[24], (DEPTH, D_MODEL, D_MODEL), D_MODEL ** -0.5),
        'w_ple_proj': nrm(ks[25], (DEPTH, PLE_DIM, D_MODEL), PLE_DIM ** -0.5),
    }


def reference(x_prompt, x_sample, cache_k, cache_v, state_pool, page_table, p_prompt, p_sample,
              g_mix, w_in, g_q, g_k, b_sb, w_pool, s_pool, w_out, g_ffn,
              w_router_group, b_router_group, w_router_expert, b_router_expert,
              w_gate, w_up, w_down, g_ple, w_ple_gate, w_ple_proj):
    n_pages = page_table.shape[1]
    past_len = n_pages * PAGE_SIZE
    dec_b, dec_t, _ = x_sample.shape
    hp, hs = x_prompt, x_sample
    kp_list, vp_list, pp_list, ks_list, vs_list, ps_list = [], [], [], [], [], []
    for i in range(DEPTH):
        tail = (w_out[i], g_ffn[i], w_router_group[i], b_router_group[i], w_router_expert[i], b_router_expert[i],
                w_gate[i], w_up[i], w_down[i], g_ple[i], w_ple_gate[i], w_ple_proj[i])
        q, k, v, u = project(hp, g_mix[i], w_in[i], g_q[i], g_k[i])
        attn = stick_breaking_prompt(q, k, v, b_sb[i])
        pool = pool_mix(u, u.shape[1], 0, w_pool[i], s_pool[i])
        hp = finish(hp, attn, pool, p_prompt[i], *tail)
        kp_list.append(k)
        vp_list.append(v)
        pp_list.append(u[:, -POOL_STATE:])
        q, k, v, u = project(hs, g_mix[i], w_in[i], g_q[i], g_k[i])
        k_past = cache_k[i][page_table].reshape(dec_b, past_len, N_HEADS, HEAD_DIM).astype(k.dtype)
        v_past = cache_v[i][page_table].reshape(dec_b, past_len, N_HEADS, HEAD_DIM).astype(v.dtype)
        k_all = jnp.concatenate([k_past, k], axis=1)
        v_all = jnp.concatenate([v_past, v], axis=1)
        attn = stick_breaking(q, k_all, v_all, past_len + jnp.arange(dec_t), jnp.arange(past_len + dec_t), b_sb[i])
        ext = jnp.concatenate([state_pool[i].astype(u.dtype), u], axis=1)
        pool = pool_mix(ext, dec_t, past_len, w_pool[i], s_pool[i])
        hs = finish(hs, attn, pool, p_sample[i], *tail)
        ks_list.append(k)
        vs_list.append(v)
        ps_list.append(ext[:, -POOL_STATE:])
    k_prompt = jnp.stack(kp_list)
    v_prompt = jnp.stack(vp_list)
    pool_prompt = jnp.stack(pp_list)
    k_sample = jnp.stack(ks_list)
    v_sample = jnp.stack(vs_list)
    pool_sample = jnp.stack(ps_list)
    return (hp, hs, k_prompt, v_prompt, pool_prompt, k_sample, v_sample, pool_sample)
```

```python
import functools
import math

import numpy as np
import jax
import jax.numpy as jnp
from jax import lax
from jax.experimental import pallas as pl
from jax.experimental.pallas import tpu as pltpu

F32 = jnp.float32
BF16 = jnp.bfloat16

D_MODEL = 1024
ATTN_WIDTH = 512
POOL_WIDTH = 512
HEAD_DIM = 64
N_HEADS = 8
POOL_WINDOWS = (2, 4, 8, 16)
POOL_GROUP_WIDTH = 128
POOL_STATE = 15
PAGE_SIZE = 128
N_EXPERT_GROUPS = 4
EXPERTS_PER_GROUP = 8
N_EXPERTS = 32
EXPERT_HIDDEN = 512
PLE_DIM = 256
RMS_EPS = 1e-6
SB_SCALE = 1.0 / math.sqrt(HEAD_DIM)
LOG2E = 1.4426950408889634

LANES = 128
HALO = 16
MOE_BLOCK = 256
VMEM_LIMIT = 56 * 1024 * 1024


def _cparams(*sem):
    return pltpu.CompilerParams(dimension_semantics=sem, vmem_limit_bytes=VMEM_LIMIT)


def _rms(x, g):
    ms = jnp.mean(x * x, axis=-1, keepdims=True)
    return x * lax.rsqrt(ms + RMS_EPS) * g


def _softplus2(z):
    return jnp.maximum(z, 0.0) + jnp.log2(1.0 + jnp.exp2(-jnp.abs(z)))


def _proj_kernel(x_ref, gmix_ref, win_ref, gq_ref, gk_ref, seg_ref,
                 q_ref, k_ref, kb_ref, v_ref, vb_ref, u_ref):
    xn = _rms(x_ref[...], gmix_ref[...]).astype(BF16)
    z = jnp.dot(xn, win_ref[...], preferred_element_type=F32)
    seg = seg_ref[...]

    def head_norm(t, g):
        t2 = t * t
        hi = t2.astype(BF16)
        lo = (t2 - hi.astype(F32)).astype(BF16)
        ss = jnp.dot(hi, seg, preferred_element_type=F32) + jnp.dot(lo, seg, preferred_element_type=F32)
        return t * lax.rsqrt(ss * (1.0 / HEAD_DIM) + RMS_EPS) * g

    q = head_norm(z[:, :ATTN_WIDTH], gq_ref[...])
    k = head_norm(z[:, ATTN_WIDTH:2 * ATTN_WIDTH], gk_ref[...])
    v = z[:, 2 * ATTN_WIDTH:3 * ATTN_WIDTH]
    q_ref[...] = q.astype(BF16)
    k_ref[...] = k
    kb_ref[...] = k.astype(BF16)
    v_ref[...] = v
    vb_ref[...] = v.astype(BF16)
    u_ref[...] = z[:, 3 * ATTN_WIDTH:]


def _proj(x, gmix, win_bf, gq, gk, seg, tm):
    n = x.shape[0]
    n_in = win_bf.shape[1]
    row = lambda w: pl.BlockSpec((tm, w), lambda i: (i, 0))
    full = lambda a: pl.BlockSpec(a.shape, lambda i: (0,) * a.ndim)
    return pl.pallas_call(
        _proj_kernel,
        grid=(n // tm,),
        in_specs=[row(D_MODEL), full(gmix), full(win_bf), full(gq), full(gk), full(seg)],
        out_specs=[row(ATTN_WIDTH)] * 5 + [row(POOL_WIDTH)],
        out_shape=[
            jax.ShapeDtypeStruct((n, ATTN_WIDTH), BF16),
            jax.ShapeDtypeStruct((n, ATTN_WIDTH), F32),
            jax.ShapeDtypeStruct((n, ATTN_WIDTH), BF16),
            jax.ShapeDtypeStruct((n, ATTN_WIDTH), F32),
            jax.ShapeDtypeStruct((n, ATTN_WIDTH), BF16),
            jax.ShapeDtypeStruct((n, POOL_WIDTH), F32),
        ],
        compiler_params=_cparams("arbitrary"),
        name="proj",
    )(x, gmix, win_bf, gq, gk, seg)


def _attn_kernel(bias_ref, q_ref, k_ref, v_ref, tri_ref, o_ref, c_ref, acc_ref, *, tq):
    hp = pl.program_id(1)
    i = pl.program_id(2)
    tk = tq
    lane = lax.broadcasted_iota(jnp.int32, (tq, LANES), 1)
    q2 = q_ref[...]
    zero = jnp.zeros_like(q2)
    qs = (jnp.where(lane < HEAD_DIM, q2, zero), jnp.where(lane >= HEAD_DIM, q2, zero))
    bs = (bias_ref[2 * hp], bias_ref[2 * hp + 1])
    mask = lax.broadcasted_iota(jnp.int32, (tq, tk), 1) < lax.broadcasted_iota(jnp.int32, (tq, tk), 0)

    def block(kb, diagonal):
        start = pl.multiple_of(kb * tk, tk)
        ks = k_ref[pl.ds(start, tk), :]
        vs = v_ref[pl.ds(start, tk), :]
        for h in range(2):
            s = lax.dot_general(qs[h], ks, (((1,), (1,)), ((), ())), preferred_element_type=F32)
            z = s + bs[h]
            lk = -_softplus2(z)
            lkm = jnp.where(mask, lk, 0.0) if diagonal else lk
            r = jnp.dot(lkm.astype(BF16), tri_ref[...], preferred_element_type=F32)
            tot = r[:, tk:]
            arg = z + lk + r[:, :tk]
            if not diagonal:
                c = c_ref[h]
                arg = arg + jnp.concatenate([c] * (tk // LANES), axis=1)
            a = jnp.exp2(arg)
            if diagonal:
                a = jnp.where(mask, a, 0.0)
            pv = jnp.dot(a.astype(BF16), vs, preferred_element_type=F32)
            if diagonal:
                acc_ref[h] = pv
                c_ref[h] = tot
            else:
                acc_ref[h] += pv
                c_ref[h] = c + tot

    block(i, True)

    def body(j, carry):
        block(i - 1 - j, False)
        return carry

    lax.fori_loop(0, i, body, 0)
    o_ref[...] = jnp.where(lane < HEAD_DIM, acc_ref[0], acc_ref[1]).astype(BF16)


def _attention(bias2, q, kb, vb, batch, seq, tq):
    n = q.shape[0]
    nq = seq // tq
    tri = np.zeros((tq, tq + LANES), np.float32)
    tri[:, :tq] = np.tril(np.ones((tq, tq), np.float32), -1)
    tri[:, tq:] = 1.0
    tri = jnp.asarray(tri, BF16)
    grid_spec = pltpu.PrefetchScalarGridSpec(
        num_scalar_prefetch=1,
        grid=(batch, N_HEADS // 2, nq),
        in_specs=[
            pl.BlockSpec((tq, LANES), lambda b, hp, i, bias: (b * nq + i, hp)),
            pl.BlockSpec((seq, LANES), lambda b, hp, i, bias: (b, hp)),
            pl.BlockSpec((seq, LANES), lambda b, hp, i, bias: (b, hp)),
            pl.BlockSpec(tri.shape, lambda b, hp, i, bias: (0, 0)),
        ],
        out_specs=pl.BlockSpec((tq, LANES), lambda b, hp, i, bias: (b * nq + i, hp)),
        scratch_shapes=[pltpu.VMEM((2, tq, LANES), F32), pltpu.VMEM((2, tq, LANES), F32)],
    )
    return pl.pallas_call(
        functools.partial(_attn_kernel, tq=tq),
        grid_spec=grid_spec,
        out_shape=jax.ShapeDtypeStruct((n, ATTN_WIDTH), BF16),
        compiler_params=_cparams("arbitrary", "arbitrary", "arbitrary"),
        name="attn_prompt",
    )(bias2, q, kb, vb, tri)


def _decode_kernel(pt_ref, qbd_ref, bias_ref, triu_ref, expand_ref, *rest, pages_per_step):
    k_refs = rest[:pages_per_step]
    v_refs = rest[pages_per_step:2 * pages_per_step]
    o_ref, c_ref, acc_ref = rest[2 * pages_per_step:]
    step = pl.program_id(1)

    @pl.when(step == 0)
    def _():
        c_ref[...] = jnp.zeros_like(c_ref)
        acc_ref[...] = jnp.zeros_like(acc_ref)

    qbd = qbd_ref[...]
    bias = bias_ref[...]
    for j in range(pages_per_step):
        kp = k_refs[j][...].astype(BF16)
        z = jnp.dot(kp, qbd, preferred_element_type=F32) + bias
        lk = -_softplus2(z)
        hi = lk.astype(BF16)
        lo = (lk - hi.astype(F32)).astype(BF16)
        r = jnp.dot(triu_ref[...], jnp.concatenate([hi, lo], axis=1), preferred_element_type=F32)
        later = r[:, :LANES] + r[:, LANES:]
        a = jnp.exp2(z + lk + later + c_ref[...])
        a_wide = jnp.dot(a.astype(BF16), expand_ref[...], preferred_element_type=F32)
        acc_ref[...] += a_wide * v_refs[j][...]
        c_ref[...] += jnp.sum(lk, axis=0, keepdims=True)

    @pl.when(step == pl.num_programs(1) - 1)
    def _():
        o_ref[...] = jnp.sum(acc_ref[...], axis=0, keepdims=True)


def _decode_attention(page_table, qbd, bias_row, cache_k, cache_v, pages_per_step):
    dec_b, n_pages = page_table.shape
    n_steps = n_pages // pages_per_step
    triu = jnp.asarray(np.triu(np.ones((PAGE_SIZE, PAGE_SIZE), np.float32), 1), BF16)
    expand = np.zeros((LANES, ATTN_WIDTH), np.float32)
    for h in range(N_HEADS):
        expand[h, h * HEAD_DIM:(h + 1) * HEAD_DIM] = 1.0
    expand = jnp.asarray(expand, BF16)

    def page_spec(j):
        def index(b, s, pt):
            page = n_pages - 1 - (s * pages_per_step + j)
            return (pt[b * n_pages + page], 0, 0)
        return pl.BlockSpec((None, PAGE_SIZE, ATTN_WIDTH), index)

    const = lambda a: pl.BlockSpec(a.shape, lambda b, s, pt: (0,) * a.ndim)
    grid_spec = pltpu.PrefetchScalarGridSpec(
        num_scalar_prefetch=1,
        grid=(dec_b, n_steps),
        in_specs=[pl.BlockSpec((None, ATTN_WIDTH, LANES), lambda b, s, pt: (b, 0, 0)),
                  const(bias_row), const(triu), const(expand)]
                 + [page_spec(j) for j in range(pages_per_step)] * 2,
        out_specs=pl.BlockSpec((None, 1, ATTN_WIDTH), lambda b, s, pt: (b, 0, 0)),
        scratch_shapes=[pltpu.VMEM((1, LANES), F32), pltpu.VMEM((PAGE_SIZE, ATTN_WIDTH), F32)],
    )
    out = pl.pallas_call(
        functools.partial(_decode_kernel, pages_per_step=pages_per_step),
        grid_spec=grid_spec,
        out_shape=jax.ShapeDtypeStruct((dec_b, 1, ATTN_WIDTH), F32),
        compiler_params=_cparams("arbitrary", "arbitrary"),
        name="attn_decode",
    )(page_table.reshape(-1), qbd, bias_row, triu, expand,
      *([cache_k] * pages_per_step), *([cache_v] * pages_per_step))
    return out.reshape(dec_b, ATTN_WIDTH)


def _post_common(x, attn_bf, pooled, wpool_ref, spool_ref, wout_ref, gffn_ref, wr_ref, br_ref, tri_ref,
                 cnt_sc, h_ref, m_ref, route_ref):
    tm = x.shape[0]
    ys = [jnp.dot(pooled[:, g * POOL_GROUP_WIDTH:(g + 1) * POOL_GROUP_WIDTH].astype(BF16), wpool_ref[g],
                  preferred_element_type=F32) for g in range(len(POOL_WINDOWS))]
    pool = (jnp.concatenate(ys, axis=1) * spool_ref[...]).astype(BF16)
    mixed = jnp.concatenate([attn_bf, pool], axis=1)
    h = x + jnp.dot(mixed, wout_ref[...], preferred_element_type=F32)
    h_ref[...] = h
    m = _rms(h, gffn_ref[...])
    m_ref[...] = m

    logits = jnp.dot(m, wr_ref[...], precision=lax.Precision.HIGHEST, preferred_element_type=F32) + br_ref[...]
    lane = lax.broadcasted_iota(jnp.int32, (tm, LANES), 1)
    lanef = lane.astype(F32)
    neg = -jnp.inf
    big = 1e9

    def first_argmax(vals):
        top = jnp.max(vals, axis=-1, keepdims=True)
        idx = jnp.min(jnp.where(vals == top, lanef, big), axis=-1, keepdims=True)
        return top, idx

    gmask = lane < N_EXPERT_GROUPS
    gmax, gidx = first_argmax(jnp.where(gmask, logits, neg))
    gsum = jnp.sum(jnp.where(gmask, jnp.exp(jnp.where(gmask, logits - gmax, 0.0)), 0.0), axis=-1, keepdims=True)
    g_w = 1.0 / gsum
    lo = N_EXPERT_GROUPS + EXPERTS_PER_GROUP * gidx
    le = jnp.where((lanef >= lo) & (lanef < lo + EXPERTS_PER_GROUP), logits, neg)
    m1, i1 = first_argmax(le)
    m2, i2 = first_argmax(jnp.where(lanef == i1, neg, le))
    e2 = jnp.exp(m2 - m1)
    den = 1.0 + e2
    gate1 = g_w / den
    gate2 = g_w * e2 / den
    ex1 = i1 - N_EXPERT_GROUPS
    ex2 = i2 - N_EXPERT_GROUPS

    oh1 = lanef == ex1
    oh2 = lanef == ex2
    onehot = jnp.where(oh1, 1.0, 0.0) + jnp.where(oh2, 1.0, 0.0)
    before = jnp.dot(tri_ref[...], onehot.astype(BF16), preferred_element_type=F32) + cnt_sc[...]
    rank1 = jnp.sum(jnp.where(oh1, before, 0.0), axis=-1, keepdims=True)
    rank2 = jnp.sum(jnp.where(oh2, before, 0.0), axis=-1, keepdims=True)
    cnt_sc[...] += jnp.sum(onehot, axis=0, keepdims=True)

    route = jnp.zeros((tm, LANES), F32)
    for col, val in enumerate((ex1, ex2, rank1, rank2, gate1, gate2)):
        route = jnp.where(lane == col, val, route)
    route_ref[...] = route


def _post_prompt_kernel(x_ref, attn_ref, u_ref, uprev_ref, cnt_in_ref,
                        wpool_ref, spool_ref, wout_ref, gffn_ref, wr_ref, br_ref, tri_ref,
                        h_ref, m_ref, route_ref, cnt_ref, ext_ref, cnt_sc, *, tiles_per_seq):
    i = pl.program_id(0)
    tm = x_ref.shape[0]

    @pl.when(i == 0)
    def _():
        cnt_sc[...] = cnt_in_ref[...]

    seq_tile = i % tiles_per_seq
    u = u_ref[...]
    ext_ref[:HALO, :] = jnp.where(seq_tile == 0, 0.0, uprev_ref[...])
    ext_ref[HALO:, :] = u
    pos1 = seq_tile * tm + lax.broadcasted_iota(jnp.int32, (tm, POOL_GROUP_WIDTH), 0) + 1
    parts = []
    for g, w in enumerate(POOL_WINDOWS):
        sl = slice(g * POOL_GROUP_WIDTH, (g + 1) * POOL_GROUP_WIDTH)
        ws = u[:, sl]
        for j in range(1, w):
            ws = ws + ext_ref[HALO - j:HALO - j + tm, sl]
        cnt = jnp.minimum(pos1, w).astype(F32)
        parts.append(ws / cnt - u[:, sl])
    pooled = jnp.concatenate(parts, axis=1)
    _post_common(x_ref[...], attn_ref[...], pooled, wpool_ref, spool_ref, wout_ref, gffn_ref, wr_ref, br_ref,
                 tri_ref, cnt_sc, h_ref, m_ref, route_ref)
    cnt_ref[...] = cnt_sc[...]


def _post_sample_kernel(x_ref, attn_ref, ext3_ref, cnt_in_ref,
                        wpool_ref, spool_ref, wout_ref, gffn_ref, wr_ref, br_ref, tri_ref,
                        h_ref, m_ref, route_ref, cnt_ref, cnt_sc, *, past_len):
    cnt_sc[...] = cnt_in_ref[...]
    n_rows = ext3_ref.shape[1]
    parts = []
    for g, w in enumerate(POOL_WINDOWS):
        sl = slice(g * POOL_GROUP_WIDTH, (g + 1) * POOL_GROUP_WIDTH)
        win = ext3_ref[:, n_rows - w:, sl]
        parts.append(jnp.sum(win, axis=1) / float(min(past_len + 1, w)) - ext3_ref[:, n_rows - 1, sl])
    pooled = jnp.concatenate(parts, axis=1)
    _post_common(x_ref[...], attn_ref[...], pooled, wpool_ref, spool_ref, wout_ref, gffn_ref, wr_ref, br_ref,
                 tri_ref, cnt_sc, h_ref, m_ref, route_ref)
    cnt_ref[...] = cnt_sc[...]


def _post_weights_specs(weights, nargs):
    return [pl.BlockSpec(a.shape, (lambda *_, nd=a.ndim: (0,) * nd)) for a in weights]


def _strict_lower(tm):
    return jnp.asarray(np.tril(np.ones((tm, tm), np.float32), -1), BF16)


def _post_out(n, tm):
    row = lambda w: pl.BlockSpec((tm, w), lambda i: (i, 0))
    specs = [row(D_MODEL), row(D_MODEL), row(LANES), pl.BlockSpec((1, LANES), lambda i: (0, 0))]
    shapes = [jax.ShapeDtypeStruct((n, D_MODEL), F32), jax.ShapeDtypeStruct((n, D_MODEL), F32),
              jax.ShapeDtypeStruct((n, LANES), F32), jax.ShapeDtypeStruct((1, LANES), F32)]
    return specs, shapes


def _post_prompt(x, attn, u, cnt_in, weights, seq, tm):
    n = x.shape[0]
    row = lambda w: pl.BlockSpec((tm, w), lambda i: (i, 0))
    halo_blocks = tm // HALO
    tri = _strict_lower(tm)
    out_specs, out_shape = _post_out(n, tm)
    return pl.pallas_call(
        functools.partial(_post_prompt_kernel, tiles_per_seq=seq // tm),
        grid=(n // tm,),
        in_specs=[row(D_MODEL), row(ATTN_WIDTH), row(POOL_WIDTH),
                  pl.BlockSpec((HALO, POOL_WIDTH), lambda i: (jnp.maximum(i * halo_blocks - 1, 0), 0)),
                  pl.BlockSpec((1, LANES), lambda i: (0, 0))]
                 + _post_weights_specs(weights + (tri,), 1),
        out_specs=out_specs,
        out_shape=out_shape,
        scratch_shapes=[pltpu.VMEM((tm + HALO, POOL_WIDTH), F32), pltpu.VMEM((1, LANES), F32)],
        compiler_params=_cparams("arbitrary"),
        name="post_prompt",
    )(x, attn, u, u, cnt_in, *weights, tri)


def _post_sample(x, attn, ext3, cnt_in, weights, past_len):
    n = x.shape[0]
    tri = _strict_lower(n)
    out_specs, out_shape = _post_out(n, n)
    full = lambda a: pl.BlockSpec(a.shape, lambda i, nd=a.ndim: (0,) * nd)
    return pl.pallas_call(
        functools.partial(_post_sample_kernel, past_len=past_len),
        grid=(1,),
        in_specs=[full(x), full(attn), full(ext3), full(cnt_in)] + _post_weights_specs(weights + (tri,), 1),
        out_specs=out_specs,
        out_shape=out_shape,
        scratch_shapes=[pltpu.VMEM((1, LANES), F32)],
        compiler_params=_cparams("arbitrary"),
        name="post_sample",
    )(x, attn, ext3, cnt_in, *weights, tri)


DMA_WAIT_ROWS = 256


def _row_copy(src, src_row, dst, dst_row, sem):
    return pltpu.make_async_copy(src.at[pl.ds(src_row, 1)], dst.at[pl.ds(dst_row, 1)], sem)


def _wait_rows(src, dst, sem, n_rows):
    assert n_rows % DMA_WAIT_ROWS == 0 or n_rows < DMA_WAIT_ROWS
    chunk = min(n_rows, DMA_WAIT_ROWS)
    for _ in range(n_rows // chunk):
        pltpu.make_async_copy(src.at[pl.ds(0, chunk)], dst.at[pl.ds(0, chunk)], sem).wait()


def _dispatch_kernel(dest_ref, m_ref, xs_in_ref, xs_ref, sem, *, tokens_per_step):
    del xs_in_ref
    base = pl.program_id(0) * tokens_per_step

    def body(t, carry):
        for k in range(2):
            _row_copy(m_ref, base + t, xs_ref, dest_ref[2 * t + k], sem).start()
        return carry

    lax.fori_loop(0, tokens_per_step, body, 0)
    _wait_rows(m_ref, xs_ref, sem, 2 * tokens_per_step)


def _dispatch(dest, m, xs, tokens_per_step):
    n = m.shape[0]
    dest_block = dest.shape[0] // (n // tokens_per_step)
    return pl.pallas_call(
        functools.partial(_dispatch_kernel, tokens_per_step=tokens_per_step),
        grid=(n // tokens_per_step,),
        in_specs=[pl.BlockSpec((dest_block,), lambda i: (i,), memory_space=pltpu.SMEM),
                  pl.BlockSpec(memory_space=pl.ANY),
                  pl.BlockSpec(memory_space=pl.ANY)],
        out_specs=pl.BlockSpec(memory_space=pl.ANY),
        out_shape=jax.ShapeDtypeStruct(xs.shape, xs.dtype),
        scratch_shapes=[pltpu.SemaphoreType.DMA(())],
        input_output_aliases={2: 0},
        compiler_params=_cparams("arbitrary"),
        name="moe_dispatch",
    )(dest, m, xs)


def _experts_kernel(blk_e_ref, blk_first_ref, n_used_ref, xs_ref, wg_ref, wu_ref, wd_ref, ys_ref,
                    wg_bf, wu_bf, wd_bf):
    i = pl.program_id(0)

    @pl.when(i < n_used_ref[0])
    def _():
        @pl.when(blk_first_ref[i] == 1)
        def _():
            wg_bf[...] = wg_ref[...].astype(BF16)
            wu_bf[...] = wu_ref[...].astype(BF16)
            wd_bf[...] = wd_ref[...].astype(BF16)

        x = xs_ref[...].astype(BF16)
        gate = jnp.dot(x, wg_bf[...], preferred_element_type=F32)
        up = jnp.dot(x, wu_bf[...], preferred_element_type=F32)
        hid = gate * jax.nn.sigmoid(gate) * up
        ys_ref[...] = jnp.dot(hid.astype(BF16), wd_bf[...], preferred_element_type=F32)

    @pl.when(i >= n_used_ref[0])
    def _():
        ys_ref[...] = jnp.zeros_like(ys_ref)


def _experts(blk_e, blk_first, n_used, xs, w_gate, w_up, w_down):
    n_blk = xs.shape[0] // MOE_BLOCK
    rows = lambda i, be, bf, nu: (jnp.minimum(i, nu[0] - 1), 0)
    out_rows = lambda i, be, bf, nu: (i, 0)
    wsel = lambda i, be, bf, nu: (be[i], 0, 0)
    grid_spec = pltpu.PrefetchScalarGridSpec(
        num_scalar_prefetch=3,
        grid=(n_blk,),
        in_specs=[pl.BlockSpec((MOE_BLOCK, D_MODEL), rows),
                  pl.BlockSpec((None, D_MODEL, EXPERT_HIDDEN), wsel),
                  pl.BlockSpec((None, D_MODEL, EXPERT_HIDDEN), wsel),
                  pl.BlockSpec((None, EXPERT_HIDDEN, D_MODEL), wsel)],
        out_specs=pl.BlockSpec((MOE_BLOCK, D_MODEL), out_rows),
        scratch_shapes=[pltpu.VMEM((D_MODEL, EXPERT_HIDDEN), BF16), pltpu.VMEM((D_MODEL, EXPERT_HIDDEN), BF16),
                        pltpu.VMEM((EXPERT_HIDDEN, D_MODEL), BF16)],
    )
    return pl.pallas_call(
        _experts_kernel,
        grid_spec=grid_spec,
        out_shape=jax.ShapeDtypeStruct(xs.shape, F32),
        compiler_params=_cparams("arbitrary"),
        name="moe_experts",
    )(blk_e, blk_first, n_used, xs, w_gate, w_up, w_down)


def _final_kernel(dest_ref, h_ref, route_ref, p_ref, ys_ref, gple_ref, wpg_ref, wpp_ref, o_ref,
                  ya_ref, yb_ref, sem):
    tm = h_ref.shape[0]

    def body(t, carry):
        _row_copy(ys_ref, dest_ref[2 * t], ya_ref, t, sem).start()
        _row_copy(ys_ref, dest_ref[2 * t + 1], yb_ref, t, sem).start()
        return carry

    lax.fori_loop(0, tm, body, 0)
    _wait_rows(ys_ref, ya_ref, sem, tm)
    _wait_rows(ys_ref, yb_ref, sem, tm)

    route = route_ref[...]
    lane = lax.broadcasted_iota(jnp.int32, route.shape, 1)
    g1 = jnp.sum(jnp.where(lane == 4, route, 0.0), axis=-1, keepdims=True)
    g2 = jnp.sum(jnp.where(lane == 5, route, 0.0), axis=-1, keepdims=True)
    h = h_ref[...] + (g1 * ya_ref[...] + g2 * yb_ref[...])
    n = _rms(h, gple_ref[...]).astype(BF16)
    gate = jax.nn.sigmoid(jnp.dot(n, wpg_ref[...], preferred_element_type=F32))
    proj = jnp.dot(p_ref[...].astype(BF16), wpp_ref[...], preferred_element_type=F32)
    o_ref[...] = h + gate * proj


def _final(dest, h, route, p, ys, gple, wpg_bf, wpp_bf, tm):
    n = h.shape[0]
    dest_block = dest.shape[0] // (n // tm)
    row = lambda w: pl.BlockSpec((tm, w), lambda i: (i, 0))
    full = lambda a: pl.BlockSpec(a.shape, lambda i, nd=a.ndim: (0,) * nd)
    return pl.pallas_call(
        _final_kernel,
        grid=(n // tm,),
        in_specs=[pl.BlockSpec((dest_block,), lambda i: (i,), memory_space=pltpu.SMEM),
                  row(D_MODEL), row(LANES), row(PLE_DIM),
                  pl.BlockSpec(memory_space=pl.ANY),
                  full(gple), full(wpg_bf), full(wpp_bf)],
        out_specs=row(D_MODEL),
        out_shape=jax.ShapeDtypeStruct((n, D_MODEL), F32),
        scratch_shapes=[pltpu.VMEM((tm, D_MODEL), F32), pltpu.VMEM((tm, D_MODEL), F32),
                        pltpu.SemaphoreType.DMA(())],
        compiler_params=_cparams("arbitrary"),
        name="moe_combine_ple",
    )(dest, h, route, p, ys, gple, wpg_bf, wpp_bf)


SMEM_BLOCK = 1024


def _pad_to(a, size):
    return jnp.concatenate([a, jnp.zeros((size - a.shape[0],), a.dtype)])


def kernel(x_prompt, x_sample, cache_k, cache_v, state_pool, page_table, p_prompt, p_sample, g_mix, w_in, g_q, g_k, b_sb, w_pool, s_pool, w_out, g_ffn, w_router_group, b_router_group, w_router_expert, b_router_expert, w_gate, w_up, w_down, g_ple, w_ple_gate, w_ple_proj):
    batch, seq, _ = x_prompt.shape
    dec_b, dec_t, _ = x_sample.shape
    assert dec_t == 1 and g_mix.shape[0] == 1
    n_pages = page_table.shape[1]
    past_len = n_pages * PAGE_SIZE
    n_p = batch * seq
    n_s = dec_b

    row = lambda a: a.reshape(1, -1).astype(F32)
    gmix = row(g_mix[0])
    win_bf = w_in[0].astype(BF16)
    gq = row(jnp.tile(g_q[0], N_HEADS) * (SB_SCALE * LOG2E))
    gk = row(jnp.tile(g_k[0], N_HEADS))
    bias2 = (b_sb[0] * LOG2E).astype(F32)
    seg = jnp.asarray(np.kron(np.eye(N_HEADS, dtype=np.float32), np.ones((HEAD_DIM, HEAD_DIM), np.float32)), BF16)
    w_r = jnp.zeros((D_MODEL, LANES), F32)
    w_r = w_r.at[:, :N_EXPERT_GROUPS].set(w_router_group[0])
    w_r = w_r.at[:, N_EXPERT_GROUPS:N_EXPERT_GROUPS + N_EXPERTS].set(w_router_expert[0])
    b_r = jnp.zeros((1, LANES), F32)
    b_r = b_r.at[0, :N_EXPERT_GROUPS].set(b_router_group[0])
    b_r = b_r.at[0, N_EXPERT_GROUPS:N_EXPERT_GROUPS + N_EXPERTS].set(b_router_expert[0])
    post_w = (w_pool[0].astype(BF16), row(s_pool[0]), w_out[0].astype(BF16), row(g_ffn[0]), w_r, b_r)
    gple = row(g_ple[0])
    wpg_bf = w_ple_gate[0].astype(BF16)
    wpp_bf = w_ple_proj[0].astype(BF16)

    xp = x_prompt.reshape(n_p, D_MODEL)
    q_p, k_p, kb_p, v_p, vb_p, u_p = _proj(xp, gmix, win_bf, gq, gk, seg, tm=512)
    attn_p = _attention(bias2, q_p, kb_p, vb_p, batch, seq, tq=256)
    cnt0 = jnp.zeros((1, LANES), F32)
    h_p, m_p, route_p, cnt_p = _post_prompt(xp, attn_p, u_p, cnt0, post_w, seq, tm=256)

    xs_ = x_sample.reshape(n_s, D_MODEL)
    q_s, k_s, _, v_s, _, u_s = _proj(xs_, gmix, win_bf, gq, gk, seg, tm=n_s)
    head_of_row = jnp.arange(ATTN_WIDTH)[:, None] // HEAD_DIM
    qbd = jnp.where(head_of_row == jnp.arange(LANES)[None, :], q_s[:, :, None], jnp.zeros((), BF16))
    bias_row = _pad_to(bias2, LANES).reshape(1, LANES)
    n_phys = cache_k.shape[1]
    attn_s = _decode_attention(page_table, qbd, bias_row,
                               cache_k[0].reshape(n_phys, PAGE_SIZE, ATTN_WIDTH),
                               cache_v[0].reshape(n_phys, PAGE_SIZE, ATTN_WIDTH), pages_per_step=8)
    zero_row = jnp.zeros((n_s, HALO - POOL_STATE - 1, POOL_WIDTH), F32)
    ext3 = jnp.concatenate([zero_row, state_pool[0], u_s[:, None, :]], axis=1)
    h_s, m_s, route_s, cnt = _post_sample(xs_, attn_s.astype(BF16), ext3, cnt_p, post_w, past_len)

    counts = cnt[0, :N_EXPERTS].astype(jnp.int32)
    nblk_e = (counts + MOE_BLOCK - 1) // MOE_BLOCK
    bends = jnp.cumsum(nblk_e)
    bstarts = bends - nblk_e
    pstarts = bstarts * MOE_BLOCK
    n_assign = 2 * (n_p + n_s)
    n_blk = n_assign // MOE_BLOCK + N_EXPERTS + (1 if n_assign % MOE_BLOCK else 0)
    n_used = bends[-1:]
    blk = jnp.arange(n_blk, dtype=jnp.int32)
    last_e = jnp.max(jnp.where(nblk_e > 0, jnp.arange(N_EXPERTS, dtype=jnp.int32), 0))
    blk_e = jnp.minimum(jnp.sum(blk[:, None] >= bends[None, :], axis=1).astype(jnp.int32), last_e)
    blk_first = jnp.any((blk[:, None] == bstarts[None, :]) & (nblk_e[None, :] > 0), axis=1).astype(jnp.int32)

    def dests(route):
        e = route[:, 0:2].astype(jnp.int32)
        rank = route[:, 2:4].astype(jnp.int32)
        start = jnp.sum(jnp.where(e[:, :, None] == jnp.arange(N_EXPERTS)[None, None, :], pstarts[None, None, :], 0),
                        axis=-1)
        return (start + rank).reshape(-1)

    dest_p = dests(route_p)
    dest_s = _pad_to(dests(route_s), SMEM_BLOCK)

    xs = jnp.zeros((n_blk * MOE_BLOCK, D_MODEL), F32)
    xs = _dispatch(dest_p, m_p, xs, tokens_per_step=SMEM_BLOCK // 2)
    xs = _dispatch(dest_s, m_s, xs, tokens_per_step=n_s)
    ys = _experts(blk_e, blk_first, n_used.astype(jnp.int32), xs, w_gate[0], w_up[0], w_down[0])

    y_p = _final(dest_p, h_p, route_p, p_prompt[0].reshape(n_p, PLE_DIM), ys, gple, wpg_bf, wpp_bf,
                 tm=SMEM_BLOCK // 2)
    y_s = _final(dest_s, h_s, route_s, p_sample[0].reshape(n_s, PLE_DIM), ys, gple, wpg_bf, wpp_bf, tm=n_s)

    heads = lambda a, b: a.reshape(1, b, -1, N_HEADS, HEAD_DIM)
    pool_prompt = u_p.reshape(batch, seq, POOL_WIDTH)[:, seq - POOL_STATE:][None]
    pool_sample = ext3[:, HALO - POOL_STATE:][None]
    return (y_p.reshape(batch, seq, D_MODEL), y_s.reshape(dec_b, dec_t, D_MODEL),
            heads(k_p, batch), heads(v_p, batch), pool_prompt,
            heads(k_s, dec_b), heads(v_s, dec_b), pool_sample)
```

```python
import functools
import math

import numpy as np
import jax
import jax.numpy as jnp
from jax import lax
from jax.experimental import pallas as pl
from jax.experimental.pallas import tpu as pltpu

F32 = jnp.float32
BF16 = jnp.bfloat16

D_MODEL = 1024
ATTN_WIDTH = 512
POOL_WIDTH = 512
HEAD_DIM = 64
N_HEADS = 8
POOL_WINDOWS = (2, 4, 8, 16)
POOL_GROUP_WIDTH = 128
POOL_STATE = 15
PAGE_SIZE = 128
N_EXPERT_GROUPS = 4
EXPERTS_PER_GROUP = 8
N_EXPERTS = 32
EXPERT_HIDDEN = 512
PLE_DIM = 256
RMS_EPS = 1e-6
SB_SCALE = 1.0 / math.sqrt(HEAD_DIM)
LOG2E = 1.4426950408889634

LANES = 128
HALO = 16
MOE_BLOCK = 256
VMEM_LIMIT = 56 * 1024 * 1024


def _cparams(*sem):
    return pltpu.CompilerParams(dimension_semantics=sem, vmem_limit_bytes=VMEM_LIMIT)


def _rms(x, g):
    ms = jnp.mean(x * x, axis=-1, keepdims=True)
    return x * lax.rsqrt(ms + RMS_EPS) * g


def _softplus2(z):
    return jnp.maximum(z, 0.0) + jnp.log2(1.0 + jnp.exp2(-jnp.abs(z)))


def _proj_kernel(x_ref, gmix_ref, win_ref, gq_ref, gk_ref, seg_ref,
                 q_ref, k_ref, kb_ref, v_ref, vb_ref, u_ref):
    xn = _rms(x_ref[...], gmix_ref[...]).astype(BF16)
    z = jnp.dot(xn, win_ref[...], preferred_element_type=F32)
    seg = seg_ref[...]

    def head_norm(t, g):
        t2 = t * t
        hi = t2.astype(BF16)
        lo = (t2 - hi.astype(F32)).astype(BF16)
        ss = jnp.dot(hi, seg, preferred_element_type=F32) + jnp.dot(lo, seg, preferred_element_type=F32)
        return t * lax.rsqrt(ss * (1.0 / HEAD_DIM) + RMS_EPS) * g

    q = head_norm(z[:, :ATTN_WIDTH], gq_ref[...])
    k = head_norm(z[:, ATTN_WIDTH:2 * ATTN_WIDTH], gk_ref[...])
    v = z[:, 2 * ATTN_WIDTH:3 * ATTN_WIDTH]
    q_ref[...] = q.astype(BF16)
    k_ref[...] = k
    kb_ref[...] = k.astype(BF16)
    v_ref[...] = v
    vb_ref[...] = v.astype(BF16)
    u_ref[...] = z[:, 3 * ATTN_WIDTH:]


def _proj(x, gmix, win_bf, gq, gk, seg, tm):
    n = x.shape[0]
    n_in = win_bf.shape[1]
    row = lambda w: pl.BlockSpec((tm, w), lambda i: (i, 0))
    full = lambda a: pl.BlockSpec(a.shape, lambda i: (0,) * a.ndim)
    return pl.pallas_call(
        _proj_kernel,
        grid=(n // tm,),
        in_specs=[row(D_MODEL), full(gmix), full(win_bf), full(gq), full(gk), full(seg)],
        out_specs=[row(ATTN_WIDTH)] * 5 + [row(POOL_WIDTH)],
        out_shape=[
            jax.ShapeDtypeStruct((n, ATTN_WIDTH), BF16),
            jax.ShapeDtypeStruct((n, ATTN_WIDTH), F32),
            jax.ShapeDtypeStruct((n, ATTN_WIDTH), BF16),
            jax.ShapeDtypeStruct((n, ATTN_WIDTH), F32),
            jax.ShapeDtypeStruct((n, ATTN_WIDTH), BF16),
            jax.ShapeDtypeStruct((n, POOL_WIDTH), F32),
        ],
        compiler_params=_cparams("arbitrary"),
        name="proj",
    )(x, gmix, win_bf, gq, gk, seg)


def _attn_kernel(bias_ref, q_ref, k_ref, v_ref, tri_ref, o_ref, c_ref, acc_ref, *, tq):
    hp = pl.program_id(1)
    i = pl.program_id(2)
    tk = tq
    lane = lax.broadcasted_iota(jnp.int32, (tq, LANES), 1)
    q2 = q_ref[...]
    zero = jnp.zeros_like(q2)
    qs = (jnp.where(lane < HEAD_DIM, q2, zero), jnp.where(lane >= HEAD_DIM, q2, zero))
    bs = (bias_ref[2 * hp], bias_ref[2 * hp + 1])
    mask = lax.broadcasted_iota(jnp.int32, (tq, tk), 1) < lax.broadcasted_iota(jnp.int32, (tq, tk), 0)

    def block(kb, diagonal):
        start = pl.multiple_of(kb * tk, tk)
        ks = k_ref[pl.ds(start, tk), :]
        vs = v_ref[pl.ds(start, tk), :]
        for h in range(2):
            s = lax.dot_general(qs[h], ks, (((1,), (1,)), ((), ())), preferred_element_type=F32)
            z = s + bs[h]
            lk = -_softplus2(z)
            lkm = jnp.where(mask, lk, 0.0) if diagonal else lk
            r = jnp.dot(lkm.astype(BF16), tri_ref[...], preferred_element_type=F32)
            tot = r[:, tk:]
            arg = z + lk + r[:, :tk]
            if not diagonal:
                c = c_ref[h]
                arg = arg + jnp.concatenate([c] * (tk // LANES), axis=1)
            a = jnp.exp2(arg)
            if diagonal:
                a = jnp.where(mask, a, 0.0)
            pv = jnp.dot(a.astype(BF16), vs, preferred_element_type=F32)
            if diagonal:
                acc_ref[h] = pv
                c_ref[h] = tot
            else:
                acc_ref[h] += pv
                c_ref[h] = c + tot

    block(i, True)

    def body(j, carry):
        block(i - 1 - j, False)
        return carry

    lax.fori_loop(0, i, body, 0)
    o_ref[...] = jnp.where(lane < HEAD_DIM, acc_ref[0], acc_ref[1]).astype(BF16)


def _attention(bias2, q, kb, vb, batch, seq, tq):
    n = q.shape[0]
    nq = seq // tq
    tri = np.zeros((tq, tq + LANES), np.float32)
    tri[:, :tq] = np.tril(np.ones((tq, tq), np.float32), -1)
    tri[:, tq:] = 1.0
    tri = jnp.asarray(tri, BF16)
    grid_spec = pltpu.PrefetchScalarGridSpec(
        num_scalar_prefetch=1,
        grid=(batch, N_HEADS // 2, nq),
        in_specs=[
            pl.BlockSpec((tq, LANES), lambda b, hp, i, bias: (b * nq + i, hp)),
            pl.BlockSpec((seq, LANES), lambda b, hp, i, bias: (b, hp)),
            pl.BlockSpec((seq, LANES), lambda b, hp, i, bias: (b, hp)),
            pl.BlockSpec(tri.shape, lambda b, hp, i, bias: (0, 0)),
        ],
        out_specs=pl.BlockSpec((tq, LANES), lambda b, hp, i, bias: (b * nq + i, hp)),
        scratch_shapes=[pltpu.VMEM((2, tq, LANES), F32), pltpu.VMEM((2, tq, LANES), F32)],
    )
    return pl.pallas_call(
        functools.partial(_attn_kernel, tq=tq),
        grid_spec=grid_spec,
        out_shape=jax.ShapeDtypeStruct((n, ATTN_WIDTH), BF16),
        compiler_params=_cparams("arbitrary", "arbitrary", "arbitrary"),
        name="attn_prompt",
    )(bias2, q, kb, vb, tri)


def _decode_kernel(pt_ref, qbd_ref, bias_ref, triu_ref, expand_ref, *rest, pages_per_step):
    k_refs = rest[:pages_per_step]
    v_refs = rest[pages_per_step:2 * pages_per_step]
    o_ref, c_ref, acc_ref = rest[2 * pages_per_step:]
    step = pl.program_id(1)

    @pl.when(step == 0)
    def _():
        c_ref[...] = jnp.zeros_like(c_ref)
        acc_ref[...] = jnp.zeros_like(acc_ref)

    def heads_to_lanes(page_ref):
        parts = [page_ref[pl.ds(h, PAGE_SIZE, stride=N_HEADS), :] for h in range(N_HEADS)]
        return jnp.concatenate(parts, axis=1)

    qbd = qbd_ref[...]
    bias = bias_ref[...]
    for j in range(pages_per_step):
        kp = heads_to_lanes(k_refs[j]).astype(BF16)
        z = jnp.dot(kp, qbd, preferred_element_type=F32) + bias
        lk = -_softplus2(z)
        hi = lk.astype(BF16)
        lo = (lk - hi.astype(F32)).astype(BF16)
        r = jnp.dot(triu_ref[...], jnp.concatenate([hi, lo], axis=1), preferred_element_type=F32)
        later = r[:, :LANES] + r[:, LANES:]
        a = jnp.exp2(z + lk + later + c_ref[...])
        a_wide = jnp.dot(a.astype(BF16), expand_ref[...], preferred_element_type=F32)
        acc_ref[...] += a_wide * heads_to_lanes(v_refs[j])
        c_ref[...] += jnp.sum(lk, axis=0, keepdims=True)

    @pl.when(step == pl.num_programs(1) - 1)
    def _():
        o_ref[...] = jnp.sum(acc_ref[...], axis=0, keepdims=True)


def _decode_attention(page_table, qbd, bias_row, cache_k, cache_v, pages_per_step):
    dec_b, n_pages = page_table.shape
    n_steps = n_pages // pages_per_step
    triu = jnp.asarray(np.triu(np.ones((PAGE_SIZE, PAGE_SIZE), np.float32), 1), BF16)
    expand = np.zeros((LANES, ATTN_WIDTH), np.float32)
    for h in range(N_HEADS):
        expand[h, h * HEAD_DIM:(h + 1) * HEAD_DIM] = 1.0
    expand = jnp.asarray(expand, BF16)

    def page_spec(j):
        def index(b, s, pt):
            page = n_pages - 1 - (s * pages_per_step + j)
            return (pt[b * n_pages + page], 0, 0)
        return pl.BlockSpec((None, PAGE_SIZE * N_HEADS, HEAD_DIM), index)

    const = lambda a: pl.BlockSpec(a.shape, lambda b, s, pt: (0,) * a.ndim)
    grid_spec = pltpu.PrefetchScalarGridSpec(
        num_scalar_prefetch=1,
        grid=(dec_b, n_steps),
        in_specs=[pl.BlockSpec((None, ATTN_WIDTH, LANES), lambda b, s, pt: (b, 0, 0)),
                  const(bias_row), const(triu), const(expand)]
                 + [page_spec(j) for j in range(pages_per_step)] * 2,
        out_specs=pl.BlockSpec((None, 1, ATTN_WIDTH), lambda b, s, pt: (b, 0, 0)),
        scratch_shapes=[pltpu.VMEM((1, LANES), F32), pltpu.VMEM((PAGE_SIZE, ATTN_WIDTH), F32)],
    )
    out = pl.pallas_call(
        functools.partial(_decode_kernel, pages_per_step=pages_per_step),
        grid_spec=grid_spec,
        out_shape=jax.ShapeDtypeStruct((dec_b, 1, ATTN_WIDTH), F32),
        compiler_params=_cparams("arbitrary", "arbitrary"),
        name="attn_decode",
    )(page_table.reshape(-1), qbd, bias_row, triu, expand,
      *([cache_k] * pages_per_step), *([cache_v] * pages_per_step))
    return out.reshape(dec_b, ATTN_WIDTH)


def _post_common(x, attn_bf, pooled, wpool_ref, spool_ref, wout_ref, gffn_ref, wr_ref, br_ref, tri_ref,
                 cnt_sc, h_ref, m_ref, route_ref):
    tm = x.shape[0]
    ys = [jnp.dot(pooled[:, g * POOL_GROUP_WIDTH:(g + 1) * POOL_GROUP_WIDTH].astype(BF16), wpool_ref[g],
                  preferred_element_type=F32) for g in range(len(POOL_WINDOWS))]
    pool = (jnp.concatenate(ys, axis=1) * spool_ref[...]).astype(BF16)
    mixed = jnp.concatenate([attn_bf, pool], axis=1)
    h = x + jnp.dot(mixed, wout_ref[...], preferred_element_type=F32)
    h_ref[...] = h
    m = _rms(h, gffn_ref[...])
    m_ref[...] = m

    logits = jnp.dot(m, wr_ref[...], precision=lax.Precision.HIGHEST, preferred_element_type=F32) + br_ref[...]
    lane = lax.broadcasted_iota(jnp.int32, (tm, LANES), 1)
    lanef = lane.astype(F32)
    neg = -jnp.inf
    big = 1e9

    def first_argmax(vals):
        top = jnp.max(vals, axis=-1, keepdims=True)
        idx = jnp.min(jnp.where(vals == top, lanef, big), axis=-1, keepdims=True)
        return top, idx

    gmask = lane < N_EXPERT_GROUPS
    gmax, gidx = first_argmax(jnp.where(gmask, logits, neg))
    gsum = jnp.sum(jnp.where(gmask, jnp.exp(jnp.where(gmask, logits - gmax, 0.0)), 0.0), axis=-1, keepdims=True)
    g_w = 1.0 / gsum
    lo = N_EXPERT_GROUPS + EXPERTS_PER_GROUP * gidx
    le = jnp.where((lanef >= lo) & (lanef < lo + EXPERTS_PER_GROUP), logits, neg)
    m1, i1 = first_argmax(le)
    m2, i2 = first_argmax(jnp.where(lanef == i1, neg, le))
    e2 = jnp.exp(m2 - m1)
    den = 1.0 + e2
    gate1 = g_w / den
    gate2 = g_w * e2 / den
    ex1 = i1 - N_EXPERT_GROUPS
    ex2 = i2 - N_EXPERT_GROUPS

    oh1 = lanef == ex1
    oh2 = lanef == ex2
    onehot = jnp.where(oh1, 1.0, 0.0) + jnp.where(oh2, 1.0, 0.0)
    before = jnp.dot(tri_ref[...], onehot.astype(BF16), preferred_element_type=F32) + cnt_sc[...]
    rank1 = jnp.sum(jnp.where(oh1, before, 0.0), axis=-1, keepdims=True)
    rank2 = jnp.sum(jnp.where(oh2, before, 0.0), axis=-1, keepdims=True)
    cnt_sc[...] += jnp.sum(onehot, axis=0, keepdims=True)

    route = jnp.zeros((tm, LANES), F32)
    for col, val in enumerate((ex1, ex2, rank1, rank2, gate1, gate2)):
        route = jnp.where(lane == col, val, route)
    route_ref[...] = route


def _post_prompt_kernel(x_ref, attn_ref, u_ref, uprev_ref, cnt_in_ref,
                        wpool_ref, spool_ref, wout_ref, gffn_ref, wr_ref, br_ref, tri_ref,
                        h_ref, m_ref, route_ref, cnt_ref, ext_ref, cnt_sc, *, tiles_per_seq):
    i = pl.program_id(0)
    tm = x_ref.shape[0]

    @pl.when(i == 0)
    def _():
        cnt_sc[...] = cnt_in_ref[...]

    seq_tile = i % tiles_per_seq
    u = u_ref[...]
    ext_ref[:HALO, :] = jnp.where(seq_tile == 0, 0.0, uprev_ref[...])
    ext_ref[HALO:, :] = u
    pos1 = seq_tile * tm + lax.broadcasted_iota(jnp.int32, (tm, POOL_GROUP_WIDTH), 0) + 1
    parts = []
    for g, w in enumerate(POOL_WINDOWS):
        sl = slice(g * POOL_GROUP_WIDTH, (g + 1) * POOL_GROUP_WIDTH)
        ws = u[:, sl]
        for j in range(1, w):
            ws = ws + ext_ref[HALO - j:HALO - j + tm, sl]
        cnt = jnp.minimum(pos1, w).astype(F32)
        parts.append(ws / cnt - u[:, sl])
    pooled = jnp.concatenate(parts, axis=1)
    _post_common(x_ref[...], attn_ref[...], pooled, wpool_ref, spool_ref, wout_ref, gffn_ref, wr_ref, br_ref,
                 tri_ref, cnt_sc, h_ref, m_ref, route_ref)
    cnt_ref[...] = cnt_sc[...]


def _post_sample_kernel(x_ref, attn_ref, ext3_ref, cnt_in_ref,
                        wpool_ref, spool_ref, wout_ref, gffn_ref, wr_ref, br_ref, tri_ref,
                        h_ref, m_ref, route_ref, cnt_ref, cnt_sc, *, past_len):
    cnt_sc[...] = cnt_in_ref[...]
    n_rows = ext3_ref.shape[1]
    parts = []
    for g, w in enumerate(POOL_WINDOWS):
        sl = slice(g * POOL_GROUP_WIDTH, (g + 1) * POOL_GROUP_WIDTH)
        win = ext3_ref[:, n_rows - w:, sl]
        parts.append(jnp.sum(win, axis=1) / float(min(past_len + 1, w)) - ext3_ref[:, n_rows - 1, sl])
    pooled = jnp.concatenate(parts, axis=1)
    _post_common(x_ref[...], attn_ref[...], pooled, wpool_ref, spool_ref, wout_ref, gffn_ref, wr_ref, br_ref,
                 tri_ref, cnt_sc, h_ref, m_ref, route_ref)
    cnt_ref[...] = cnt_sc[...]


def _post_weights_specs(weights, nargs):
    return [pl.BlockSpec(a.shape, (lambda *_, nd=a.ndim: (0,) * nd)) for a in weights]


def _strict_lower(tm):
    return jnp.asarray(np.tril(np.ones((tm, tm), np.float32), -1), BF16)


def _post_out(n, tm):
    row = lambda w: pl.BlockSpec((tm, w), lambda i: (i, 0))
    specs = [row(D_MODEL), row(D_MODEL), row(LANES), pl.BlockSpec((1, LANES), lambda i: (0, 0))]
    shapes = [jax.ShapeDtypeStruct((n, D_MODEL), F32), jax.ShapeDtypeStruct((n, D_MODEL), F32),
              jax.ShapeDtypeStruct((n, LANES), F32), jax.ShapeDtypeStruct((1, LANES), F32)]
    return specs, shapes


def _post_prompt(x, attn, u, cnt_in, weights, seq, tm):
    n = x.shape[0]
    row = lambda w: pl.BlockSpec((tm, w), lambda i: (i, 0))
    halo_blocks = tm // HALO
    tri = _strict_lower(tm)
    out_specs, out_shape = _post_out(n, tm)
    return pl.pallas_call(
        functools.partial(_post_prompt_kernel, tiles_per_seq=seq // tm),
        grid=(n // tm,),
        in_specs=[row(D_MODEL), row(ATTN_WIDTH), row(POOL_WIDTH),
                  pl.BlockSpec((HALO, POOL_WIDTH), lambda i: (jnp.maximum(i * halo_blocks - 1, 0), 0)),
                  pl.BlockSpec((1, LANES), lambda i: (0, 0))]
                 + _post_weights_specs(weights + (tri,), 1),
        out_specs=out_specs,
        out_shape=out_shape,
        scratch_shapes=[pltpu.VMEM((tm + HALO, POOL_WIDTH), F32), pltpu.VMEM((1, LANES), F32)],
        compiler_params=_cparams("arbitrary"),
        name="post_prompt",
    )(x, attn, u, u, cnt_in, *weights, tri)


def _post_sample(x, attn, ext3, cnt_in, weights, past_len):
    n = x.shape[0]
    tri = _strict_lower(n)
    out_specs, out_shape = _post_out(n, n)
    full = lambda a: pl.BlockSpec(a.shape, lambda i, nd=a.ndim: (0,) * nd)
    return pl.pallas_call(
        functools.partial(_post_sample_kernel, past_len=past_len),
        grid=(1,),
        in_specs=[full(x), full(attn), full(ext3), full(cnt_in)] + _post_weights_specs(weights + (tri,), 1),
        out_specs=out_specs,
        out_shape=out_shape,
        scratch_shapes=[pltpu.VMEM((1, LANES), F32)],
        compiler_params=_cparams("arbitrary"),
        name="post_sample",
    )(x, attn, ext3, cnt_in, *weights, tri)


DMA_WAIT_ROWS = 256


def _row_copy(src, src_row, dst, dst_row, sem):
    return pltpu.make_async_copy(src.at[pl.ds(src_row, 1)], dst.at[pl.ds(dst_row, 1)], sem)


def _wait_rows(src, dst, sem, n_rows):
    assert n_rows % DMA_WAIT_ROWS == 0 or n_rows < DMA_WAIT_ROWS
    chunk = min(n_rows, DMA_WAIT_ROWS)
    for _ in range(n_rows // chunk):
        pltpu.make_async_copy(src.at[pl.ds(0, chunk)], dst.at[pl.ds(0, chunk)], sem).wait()


def _dispatch_kernel(dest_ref, m_ref, xs_in_ref, xs_ref, sem, *, tokens_per_step):
    del xs_in_ref

    def body(t, carry):
        for k in range(2):
            _row_copy(m_ref, t, xs_ref, dest_ref[2 * t + k], sem).start()
        return carry

    lax.fori_loop(0, tokens_per_step, body, 0)
    _wait_rows(m_ref, xs_ref, sem, 2 * tokens_per_step)


def _dispatch(dest, m, xs, tokens_per_step):
    n = m.shape[0]
    dest_block = dest.shape[0] // (n // tokens_per_step)
    return pl.pallas_call(
        functools.partial(_dispatch_kernel, tokens_per_step=tokens_per_step),
        grid=(n // tokens_per_step,),
        in_specs=[pl.BlockSpec((dest_block,), lambda i: (i,), memory_space=pltpu.SMEM),
                  pl.BlockSpec((tokens_per_step, D_MODEL), lambda i: (i, 0)),
                  pl.BlockSpec(memory_space=pl.ANY)],
        out_specs=pl.BlockSpec(memory_space=pl.ANY),
        out_shape=jax.ShapeDtypeStruct(xs.shape, xs.dtype),
        scratch_shapes=[pltpu.SemaphoreType.DMA(())],
        input_output_aliases={2: 0},
        compiler_params=_cparams("arbitrary"),
        name="moe_dispatch",
    )(dest, m, xs)


def _experts_kernel(blk_e_ref, blk_first_ref, n_used_ref, xs_ref, wg_ref, wu_ref, wd_ref, ys_ref,
                    wg_bf, wu_bf, wd_bf):
    i = pl.program_id(0)

    @pl.when(i < n_used_ref[0])
    def _():
        @pl.when(blk_first_ref[i] == 1)
        def _():
            wg_bf[...] = wg_ref[...].astype(BF16)
            wu_bf[...] = wu_ref[...].astype(BF16)
            wd_bf[...] = wd_ref[...].astype(BF16)

        x = xs_ref[...].astype(BF16)
        gate = jnp.dot(x, wg_bf[...], preferred_element_type=F32)
        up = jnp.dot(x, wu_bf[...], preferred_element_type=F32)
        hid = gate * jax.nn.sigmoid(gate) * up
        ys_ref[...] = jnp.dot(hid.astype(BF16), wd_bf[...], preferred_element_type=F32)

    @pl.when(i >= n_used_ref[0])
    def _():
        ys_ref[...] = jnp.zeros_like(ys_ref)


def _experts(blk_e, blk_first, n_used, xs, w_gate, w_up, w_down):
    n_blk = xs.shape[0] // MOE_BLOCK
    rows = lambda i, be, bf, nu: (jnp.minimum(i, nu[0] - 1), 0)
    out_rows = lambda i, be, bf, nu: (i, 0)
    wsel = lambda i, be, bf, nu: (be[i], 0, 0)
    grid_spec = pltpu.PrefetchScalarGridSpec(
        num_scalar_prefetch=3,
        grid=(n_blk,),
        in_specs=[pl.BlockSpec((MOE_BLOCK, D_MODEL), rows),
                  pl.BlockSpec((None, D_MODEL, EXPERT_HIDDEN), wsel),
                  pl.BlockSpec((None, D_MODEL, EXPERT_HIDDEN), wsel),
                  pl.BlockSpec((None, EXPERT_HIDDEN, D_MODEL), wsel)],
        out_specs=pl.BlockSpec((MOE_BLOCK, D_MODEL), out_rows),
        scratch_shapes=[pltpu.VMEM((D_MODEL, EXPERT_HIDDEN), BF16), pltpu.VMEM((D_MODEL, EXPERT_HIDDEN), BF16),
                        pltpu.VMEM((EXPERT_HIDDEN, D_MODEL), BF16)],
    )
    return pl.pallas_call(
        _experts_kernel,
        grid_spec=grid_spec,
        out_shape=jax.ShapeDtypeStruct(xs.shape, F32),
        compiler_params=_cparams("arbitrary"),
        name="moe_experts",
    )(blk_e, blk_first, n_used, xs, w_gate, w_up, w_down)


def _final_kernel(dest_ref, h_ref, route_ref, p_ref, ys_ref, gple_ref, wpg_ref, wpp_ref, o_ref,
                  ya_ref, yb_ref, sem):
    tm = h_ref.shape[0]

    def body(t, carry):
        _row_copy(ys_ref, dest_ref[2 * t], ya_ref, t, sem).start()
        _row_copy(ys_ref, dest_ref[2 * t + 1], yb_ref, t, sem).start()
        return carry

    lax.fori_loop(0, tm, body, 0)
    _wait_rows(ys_ref, ya_ref, sem, tm)
    _wait_rows(ys_ref, yb_ref, sem, tm)

    route = route_ref[...]
    lane = lax.broadcasted_iota(jnp.int32, route.shape, 1)
    g1 = jnp.sum(jnp.where(lane == 4, route, 0.0), axis=-1, keepdims=True)
    g2 = jnp.sum(jnp.where(lane == 5, route, 0.0), axis=-1, keepdims=True)
    h = h_ref[...] + (g1 * ya_ref[...] + g2 * yb_ref[...])
    n = _rms(h, gple_ref[...]).astype(BF16)
    gate = jax.nn.sigmoid(jnp.dot(n, wpg_ref[...], preferred_element_type=F32))
    proj = jnp.dot(p_ref[...].astype(BF16), wpp_ref[...], preferred_element_type=F32)
    o_ref[...] = h + gate * proj


def _final(dest, h, route, p, ys, gple, wpg_bf, wpp_bf, tm):
    n = h.shape[0]
    dest_block = dest.shape[0] // (n // tm)
    row = lambda w: pl.BlockSpec((tm, w), lambda i: (i, 0))
    full = lambda a: pl.BlockSpec(a.shape, lambda i, nd=a.ndim: (0,) * nd)
    return pl.pallas_call(
        _final_kernel,
        grid=(n // tm,),
        in_specs=[pl.BlockSpec((dest_block,), lambda i: (i,), memory_space=pltpu.SMEM),
                  row(D_MODEL), row(LANES), row(PLE_DIM),
                  pl.BlockSpec(memory_space=pl.ANY),
                  full(gple), full(wpg_bf), full(wpp_bf)],
        out_specs=row(D_MODEL),
        out_shape=jax.ShapeDtypeStruct((n, D_MODEL), F32),
        scratch_shapes=[pltpu.VMEM((tm, D_MODEL), F32), pltpu.VMEM((tm, D_MODEL), F32),
                        pltpu.SemaphoreType.DMA(())],
        compiler_params=_cparams("arbitrary"),
        name="moe_combine_ple",
    )(dest, h, route, p, ys, gple, wpg_bf, wpp_bf)


SMEM_BLOCK = 1024


def _pad_to(a, size):
    return jnp.concatenate([a, jnp.zeros((size - a.shape[0],), a.dtype)])


def kernel(x_prompt, x_sample, cache_k, cache_v, state_pool, page_table, p_prompt, p_sample, g_mix, w_in, g_q, g_k, b_sb, w_pool, s_pool, w_out, g_ffn, w_router_group, b_router_group, w_router_expert, b_router_expert, w_gate, w_up, w_down, g_ple, w_ple_gate, w_ple_proj):
    batch, seq, _ = x_prompt.shape
    dec_b, dec_t, _ = x_sample.shape
    assert dec_t == 1 and g_mix.shape[0] == 1
    n_pages = page_table.shape[1]
    past_len = n_pages * PAGE_SIZE
    n_p = batch * seq
    n_s = dec_b

    row = lambda a: a.reshape(1, -1).astype(F32)
    gmix = row(g_mix[0])
    win_bf = w_in[0].astype(BF16)
    gq = row(jnp.tile(g_q[0], N_HEADS) * (SB_SCALE * LOG2E))
    gk = row(jnp.tile(g_k[0], N_HEADS))
    bias2 = (b_sb[0] * LOG2E).astype(F32)
    seg = jnp.asarray(np.kron(np.eye(N_HEADS, dtype=np.float32), np.ones((HEAD_DIM, HEAD_DIM), np.float32)), BF16)
    w_r = jnp.zeros((D_MODEL, LANES), F32)
    w_r = w_r.at[:, :N_EXPERT_GROUPS].set(w_router_group[0])
    w_r = w_r.at[:, N_EXPERT_GROUPS:N_EXPERT_GROUPS + N_EXPERTS].set(w_router_expert[0])
    b_r = jnp.zeros((1, LANES), F32)
    b_r = b_r.at[0, :N_EXPERT_GROUPS].set(b_router_group[0])
    b_r = b_r.at[0, N_EXPERT_GROUPS:N_EXPERT_GROUPS + N_EXPERTS].set(b_router_expert[0])
    post_w = (w_pool[0].astype(BF16), row(s_pool[0]), w_out[0].astype(BF16), row(g_ffn[0]), w_r, b_r)
    gple = row(g_ple[0])
    wpg_bf = w_ple_gate[0].astype(BF16)
    wpp_bf = w_ple_proj[0].astype(BF16)

    xp = x_prompt.reshape(n_p, D_MODEL)
    q_p, k_p, kb_p, v_p, vb_p, u_p = _proj(xp, gmix, win_bf, gq, gk, seg, tm=512)
    attn_p = _attention(bias2, q_p, kb_p, vb_p, batch, seq, tq=256)
    cnt0 = jnp.zeros((1, LANES), F32)
    h_p, m_p, route_p, cnt_p = _post_prompt(xp, attn_p, u_p, cnt0, post_w, seq, tm=256)

    xs_ = x_sample.reshape(n_s, D_MODEL)
    q_s, k_s, _, v_s, _, u_s = _proj(xs_, gmix, win_bf, gq, gk, seg, tm=n_s)
    head_of_row = jnp.arange(ATTN_WIDTH)[:, None] // HEAD_DIM
    qbd = jnp.where(head_of_row == jnp.arange(LANES)[None, :], q_s[:, :, None], jnp.zeros((), BF16))
    bias_row = _pad_to(bias2, LANES).reshape(1, LANES)
    n_phys = cache_k.shape[1]
    attn_s = _decode_attention(page_table, qbd, bias_row,
                               cache_k[0].reshape(n_phys, PAGE_SIZE * N_HEADS, HEAD_DIM),
                               cache_v[0].reshape(n_phys, PAGE_SIZE * N_HEADS, HEAD_DIM), pages_per_step=8)
    zero_row = jnp.zeros((n_s, HALO - POOL_STATE - 1, POOL_WIDTH), F32)
    ext3 = jnp.concatenate([zero_row, state_pool[0], u_s[:, None, :]], axis=1)
    h_s, m_s, route_s, cnt = _post_sample(xs_, attn_s.astype(BF16), ext3, cnt_p, post_w, past_len)

    counts = cnt[0, :N_EXPERTS].astype(jnp.int32)
    nblk_e = (counts + MOE_BLOCK - 1) // MOE_BLOCK
    bends = jnp.cumsum(nblk_e)
    bstarts = bends - nblk_e
    pstarts = bstarts * MOE_BLOCK
    n_assign = 2 * (n_p + n_s)
    n_blk = n_assign // MOE_BLOCK + N_EXPERTS + (1 if n_assign % MOE_BLOCK else 0)
    n_used = bends[-1:]
    blk = jnp.arange(n_blk, dtype=jnp.int32)
    last_e = jnp.max(jnp.where(nblk_e > 0, jnp.arange(N_EXPERTS, dtype=jnp.int32), 0))
    blk_e = jnp.minimum(jnp.sum(blk[:, None] >= bends[None, :], axis=1).astype(jnp.int32), last_e)
    blk_first = jnp.any((blk[:, None] == bstarts[None, :]) & (nblk_e[None, :] > 0), axis=1).astype(jnp.int32)

    def dests(route):
        e = route[:, 0:2].astype(jnp.int32)
        rank = route[:, 2:4].astype(jnp.int32)
        start = jnp.sum(jnp.where(e[:, :, None] == jnp.arange(N_EXPERTS)[None, None, :], pstarts[None, None, :], 0),
                        axis=-1)
        return (start + rank).reshape(-1)

    dest_p = dests(route_p)
    dest_s = _pad_to(dests(route_s), SMEM_BLOCK)

    xs = jnp.zeros((n_blk * MOE_BLOCK, D_MODEL), F32)
    xs = _dispatch(dest_p, m_p, xs, tokens_per_step=SMEM_BLOCK // 2)
    xs = _dispatch(dest_s, m_s, xs, tokens_per_step=n_s)
    ys = _experts(blk_e, blk_first, n_used.astype(jnp.int32), xs, w_gate[0], w_up[0], w_down[0])

    y_p = _final(dest_p, h_p, route_p, p_prompt[0].reshape(n_p, PLE_DIM), ys, gple, wpg_bf, wpp_bf,
                 tm=SMEM_BLOCK // 2)
    y_s = _final(dest_s, h_s, route_s, p_sample[0].reshape(n_s, PLE_DIM), ys, gple, wpg_bf, wpp_bf, tm=n_s)

    heads = lambda a, b: a.reshape(1, b, -1, N_HEADS, HEAD_DIM)
    pool_prompt = u_p.reshape(batch, seq, POOL_WIDTH)[:, seq - POOL_STATE:][None]
    pool_sample = ext3[:, HALO - POOL_STATE:][None]
    return (y_p.reshape(batch, seq, D_MODEL), y_s.reshape(dec_b, dec_t, D_MODEL),
            heads(k_p, batch), heads(v_p, batch), pool_prompt,
            heads(k_s, dec_b), heads(v_s, dec_b), pool_sample)
```

```python
import functools
import math

import numpy as np
import jax
import jax.numpy as jnp
from jax import lax
from jax.experimental import pallas as pl
from jax.experimental.pallas import tpu as pltpu

F32 = jnp.float32
BF16 = jnp.bfloat16

D_MODEL = 1024
ATTN_WIDTH = 512
POOL_WIDTH = 512
HEAD_DIM = 64
N_HEADS = 8
POOL_WINDOWS = (2, 4, 8, 16)
POOL_GROUP_WIDTH = 128
POOL_STATE = 15
PAGE_SIZE = 128
N_EXPERT_GROUPS = 4
EXPERTS_PER_GROUP = 8
N_EXPERTS = 32
EXPERT_HIDDEN = 512
PLE_DIM = 256
RMS_EPS = 1e-6
SB_SCALE = 1.0 / math.sqrt(HEAD_DIM)
LOG2E = 1.4426950408889634

LANES = 128
HALO = 16
MOE_BLOCK = 256
VMEM_LIMIT = 56 * 1024 * 1024


def _cparams(*sem):
    return pltpu.CompilerParams(dimension_semantics=sem, vmem_limit_bytes=VMEM_LIMIT)


def _rms(x, g):
    ms = jnp.mean(x * x, axis=-1, keepdims=True)
    return x * lax.rsqrt(ms + RMS_EPS) * g


def _softplus2(z):
    return jnp.maximum(z, 0.0) + jnp.log2(1.0 + jnp.exp2(-jnp.abs(z)))


def _proj_kernel(x_ref, gmix_ref, win_ref, gq_ref, gk_ref, seg_ref, q_ref, *out_refs, key_block):
    xn = _rms(x_ref[...], gmix_ref[...]).astype(BF16)
    z = jnp.dot(xn, win_ref[...], preferred_element_type=F32)
    seg = seg_ref[...]

    def head_norm(t, g):
        t2 = t * t
        hi = t2.astype(BF16)
        lo = (t2 - hi.astype(F32)).astype(BF16)
        ss = jnp.dot(hi, seg, preferred_element_type=F32) + jnp.dot(lo, seg, preferred_element_type=F32)
        return t * lax.rsqrt(ss * (1.0 / HEAD_DIM) + RMS_EPS) * g

    q = head_norm(z[:, :ATTN_WIDTH], gq_ref[...])
    k = head_norm(z[:, ATTN_WIDTH:2 * ATTN_WIDTH], gk_ref[...])
    v = z[:, 2 * ATTN_WIDTH:3 * ATTN_WIDTH]
    q_ref[...] = q.astype(BF16)
    if key_block is None:
        k_ref, v_ref, u_ref = out_refs
        k_ref[...] = k
        v_ref[...] = v
    else:
        kt_ref, ktb_ref, vt_ref, vb_ref, u_ref = out_refs
        kt = k.T
        kt_ref[...] = kt
        ktb = kt.astype(BF16)
        for c in range(ktb_ref.shape[0]):
            ktb_ref[c] = ktb[:, c * key_block:(c + 1) * key_block]
        vt_ref[...] = v.T
        vb_ref[...] = v.astype(BF16)
    u_ref[...] = z[:, 3 * ATTN_WIDTH:]


def _proj(x, gmix, win_bf, gq, gk, seg, tm, seq=None, key_block=None):
    n = x.shape[0]
    row = lambda w: pl.BlockSpec((tm, w), lambda i: (i, 0))
    full = lambda a: pl.BlockSpec(a.shape, lambda i: (0,) * a.ndim)
    rows_f32 = jax.ShapeDtypeStruct((n, ATTN_WIDTH), F32)
    rows_bf16 = jax.ShapeDtypeStruct((n, ATTN_WIDTH), BF16)
    if key_block is None:
        out_specs = [row(ATTN_WIDTH)] * 3 + [row(POOL_WIDTH)]
        out_shape = [rows_bf16, rows_f32, rows_f32, jax.ShapeDtypeStruct((n, POOL_WIDTH), F32)]
    else:
        batch, tiles = n // seq, seq // tm
        transposed = pl.BlockSpec((None, ATTN_WIDTH, tm), lambda i: (i // tiles, 0, i % tiles))
        t_shape = jax.ShapeDtypeStruct((batch, ATTN_WIDTH, seq), F32)
        blocks = tm // key_block
        out_specs = [row(ATTN_WIDTH), transposed,
                     pl.BlockSpec((None, blocks, ATTN_WIDTH, key_block), lambda i: (i // tiles, i % tiles, 0, 0)),
                     transposed, row(ATTN_WIDTH), row(POOL_WIDTH)]
        out_shape = [rows_bf16, t_shape,
                     jax.ShapeDtypeStruct((batch, seq // key_block, ATTN_WIDTH, key_block), BF16),
                     t_shape, rows_bf16, jax.ShapeDtypeStruct((n, POOL_WIDTH), F32)]
    return pl.pallas_call(
        functools.partial(_proj_kernel, key_block=key_block),
        grid=(n // tm,),
        in_specs=[row(D_MODEL), full(gmix), full(win_bf), full(gq), full(gk), full(seg)],
        out_specs=out_specs,
        out_shape=out_shape,
        compiler_params=_cparams("arbitrary"),
        name="proj",
    )(x, gmix, win_bf, gq, gk, seg)


ROW_CHUNK = 128


def _attn_kernel(bias_ref, q_ref, k_ref, v_ref, ntri_ref, o_ref, qm_ref, argp_ref, a_ref, totp_ref, c_ref, acc_ref,
                 *, tq, tk):
    hp = pl.program_id(1)
    i = pl.program_id(2)
    diag_blocks = tq // tk
    lane = lax.broadcasted_iota(jnp.int32, (tq, LANES), 1)
    q2 = q_ref[...]
    zero = jnp.zeros_like(q2)
    qm_ref[0] = jnp.where(lane < HEAD_DIM, q2, zero)
    qm_ref[1] = jnp.where(lane >= HEAD_DIM, q2, zero)
    bs = (bias_ref[2 * hp], bias_ref[2 * hp + 1])
    sign = jnp.int32(-2 ** 31)
    chunks = [slice(r, r + ROW_CHUNK) for r in range(0, tq, ROW_CHUNK)]

    def causal(rs, d):
        row = lax.broadcasted_iota(jnp.int32, (ROW_CHUNK, tk), 0) + rs.start
        col = lax.broadcasted_iota(jnp.int32, (ROW_CHUNK, tk), 1) + d * tk
        return col < row

    def scores(ks, rs):
        return [jnp.dot(qm_ref[h, rs], ks, preferred_element_type=F32) + bs[h] for h in range(2)]

    def suffix_sums(zs, rs, mask):
        for h, z in enumerate(zs):
            neg_abs = lax.bitcast_convert_type(lax.bitcast_convert_type(z, jnp.int32) | sign, F32)
            sp = jnp.maximum(z, 0.0) + jnp.log2(1.0 + jnp.exp2(neg_abs))
            spm = sp if mask is None else jnp.where(mask, sp, 0.0)
            later = jnp.dot(spm.astype(BF16), ntri_ref[...], preferred_element_type=F32)
            argp_ref[h, rs] = (z - sp) + later
            totp_ref[h, rs] = jnp.broadcast_to(jnp.sum(spm, axis=-1, keepdims=True), (ROW_CHUNK, LANES))

    def weights(rs, mask, first):
        for h in range(2):
            arg = argp_ref[h, rs]
            if not first:
                arg = arg - jnp.concatenate([c_ref[h, rs]] * (tk // LANES), axis=1)
            a = jnp.exp2(arg)
            if mask is not None:
                a = jnp.where(mask, a, 0.0)
            a_ref[h, rs] = a.astype(BF16)

    def weigh_values(vs, rs, first):
        for h in range(2):
            pv = jnp.dot(a_ref[h, rs], vs, preferred_element_type=F32)
            if first:
                acc_ref[h, rs] = pv
                c_ref[h, rs] = totp_ref[h, rs]
            else:
                acc_ref[h, rs] += pv
                c_ref[h, rs] += totp_ref[h, rs]

    def values(kb):
        return v_ref[pl.ds(pl.multiple_of(kb * tk, tk), tk), :]

    for d in reversed(range(diag_blocks)):
        kb = i * diag_blocks + d
        first = d == diag_blocks - 1
        ks, vs = k_ref[kb], values(kb)
        for rs in chunks:
            if rs.start + ROW_CHUNK - 1 <= d * tk:
                if first:
                    acc_ref[:, rs] = jnp.zeros((2, ROW_CHUNK, LANES), F32)
                    c_ref[:, rs] = jnp.zeros((2, ROW_CHUNK, LANES), F32)
                continue
            mask = causal(rs, d)
            suffix_sums(scores(ks, rs), rs, mask)
            weights(rs, mask, first)
            weigh_values(vs, rs, first)

    n_full = i * diag_blocks

    @pl.when(n_full > 0)
    def _():
        ks0 = k_ref[n_full - 1]
        for rs in chunks:
            suffix_sums(scores(ks0, rs), rs, None)

        def body(j, carry):
            cur = n_full - 1 - j
            ks, vs = k_ref[cur - 1], values(cur)
            zs = []
            for rs in chunks:
                weights(rs, None, first=False)
                weigh_values(vs, rs, first=False)
                zs.append(scores(ks, rs))
            for rs, z in zip(chunks, zs):
                suffix_sums(z, rs, None)
            return carry

        lax.fori_loop(0, n_full - 1, body, 0)
        vs0 = values(0)
        for rs in chunks:
            weights(rs, None, first=False)
            weigh_values(vs0, rs, first=False)

    o_ref[...] = jnp.where(lane < HEAD_DIM, acc_ref[0], acc_ref[1]).astype(BF16)


def _attention(bias2, q, ktb, vb, batch, seq, tq, tk):
    n = q.shape[0]
    nq = seq // tq
    nk = seq // tk
    ntri = jnp.asarray(-np.tril(np.ones((tk, tk), np.float32), -1), BF16)
    grid_spec = pltpu.PrefetchScalarGridSpec(
        num_scalar_prefetch=1,
        grid=(batch, N_HEADS // 2, nq),
        in_specs=[
            pl.BlockSpec((tq, LANES), lambda b, hp, i, bias: (b * nq + i, hp)),
            pl.BlockSpec((None, nk, LANES, tk), lambda b, hp, i, bias: (b, 0, hp, 0)),
            pl.BlockSpec((seq, LANES), lambda b, hp, i, bias: (b, hp)),
            pl.BlockSpec(ntri.shape, lambda b, hp, i, bias: (0, 0)),
        ],
        out_specs=pl.BlockSpec((tq, LANES), lambda b, hp, i, bias: (b * nq + i, hp)),
        scratch_shapes=[pltpu.VMEM((2, tq, LANES), BF16), pltpu.VMEM((2, tq, tk), F32), pltpu.VMEM((2, tq, tk), BF16),
                        pltpu.VMEM((2, tq, LANES), F32), pltpu.VMEM((2, tq, LANES), F32),
                        pltpu.VMEM((2, tq, LANES), F32)],
    )
    return pl.pallas_call(
        functools.partial(_attn_kernel, tq=tq, tk=tk),
        grid_spec=grid_spec,
        out_shape=jax.ShapeDtypeStruct((n, ATTN_WIDTH), BF16),
        compiler_params=_cparams("arbitrary", "arbitrary", "arbitrary"),
        name="attn_prompt",
    )(bias2, q, ktb, vb, ntri)


def _decode_kernel(pt_ref, q_ref, bias_ref, tri_ref, *rest, pages_per_step):
    k_refs = rest[:pages_per_step]
    v_refs = rest[pages_per_step:2 * pages_per_step]
    o_ref, c_ref, acc_ref = rest[2 * pages_per_step:]
    step = pl.program_id(1)

    @pl.when(step == 0)
    def _():
        c_ref[...] = jnp.zeros_like(c_ref)
        acc_ref[...] = jnp.zeros_like(acc_ref)

    head_row = lax.broadcasted_iota(jnp.int32, (N_HEADS, PAGE_SIZE), 0)
    zs = []
    for j in range(pages_per_step):
        zj = jnp.zeros((N_HEADS, PAGE_SIZE), F32)
        for h in range(N_HEADS):
            dims = slice(h * HEAD_DIM, (h + 1) * HEAD_DIM)
            zrow = jnp.sum(q_ref[dims, :] * k_refs[j][h], axis=0, keepdims=True)
            zj = jnp.where(head_row == h, zrow, zj)
        zs.append(zj)
    z = jnp.concatenate(zs, axis=0) + bias_ref[...]
    lk = -_softplus2(z)
    hi = lk.astype(BF16)
    lo = (lk - hi.astype(F32)).astype(BF16)
    r = (jnp.dot(hi, tri_ref[...], preferred_element_type=F32)
         + jnp.dot(lo, tri_ref[...], preferred_element_type=F32))
    tot = r[:, PAGE_SIZE:]
    c = c_ref[...]
    cs = []
    for j in range(pages_per_step):
        cs.append(c)
        c = c + tot[j * N_HEADS:(j + 1) * N_HEADS]
    c_ref[...] = c
    a = jnp.exp2(z + lk + r[:, :PAGE_SIZE] + jnp.concatenate(cs, axis=0))

    for h in range(N_HEADS):
        dims = slice(h * HEAD_DIM, (h + 1) * HEAD_DIM)
        acc = acc_ref[dims, :]
        for j in range(pages_per_step):
            row = j * N_HEADS + h
            acc = acc + a[row:row + 1, :] * v_refs[j][h]
        acc_ref[dims, :] = acc

    @pl.when(step == pl.num_programs(1) - 1)
    def _():
        o_ref[...] = jnp.sum(acc_ref[...], axis=1, keepdims=True)


def _decode_attention(page_table, q_lanes, bias_rows, cache_kt, cache_vt, pages_per_step):
    dec_b, n_pages = page_table.shape
    n_steps = n_pages // pages_per_step
    tri = np.zeros((PAGE_SIZE, 2 * PAGE_SIZE), np.float32)
    tri[:, :PAGE_SIZE] = np.tril(np.ones((PAGE_SIZE, PAGE_SIZE), np.float32), -1)
    tri[:, PAGE_SIZE:] = 1.0
    tri = jnp.asarray(tri, BF16)

    def page_spec(j):
        def index(b, s, pt):
            page = n_pages - 1 - (s * pages_per_step + j)
            return (pt[b * n_pages + page], 0, 0, 0)
        return pl.BlockSpec((None, N_HEADS, HEAD_DIM, PAGE_SIZE), index)

    const = lambda a: pl.BlockSpec(a.shape, lambda b, s, pt: (0,) * a.ndim)
    grid_spec = pltpu.PrefetchScalarGridSpec(
        num_scalar_prefetch=1,
        grid=(dec_b, n_steps),
        in_specs=[pl.BlockSpec((None, ATTN_WIDTH, PAGE_SIZE), lambda b, s, pt: (b, 0, 0)),
                  const(bias_rows), const(tri)]
                 + [page_spec(j) for j in range(pages_per_step)] * 2,
        out_specs=pl.BlockSpec((None, ATTN_WIDTH, 1), lambda b, s, pt: (b, 0, 0)),
        scratch_shapes=[pltpu.VMEM((N_HEADS, PAGE_SIZE), F32), pltpu.VMEM((ATTN_WIDTH, PAGE_SIZE), F32)],
    )
    out = pl.pallas_call(
        functools.partial(_decode_kernel, pages_per_step=pages_per_step),
        grid_spec=grid_spec,
        out_shape=jax.ShapeDtypeStruct((dec_b, ATTN_WIDTH, 1), F32),
        compiler_params=_cparams("arbitrary", "arbitrary"),
        name="attn_decode",
    )(page_table.reshape(-1), q_lanes, bias_rows, tri,
      *([cache_kt] * pages_per_step), *([cache_vt] * pages_per_step))
    return out.reshape(dec_b, ATTN_WIDTH)


def _post_common(x, attn_bf, pooled, wpool_ref, spool_ref, wout_ref, gffn_ref, wr_ref, br_ref, tri_ref,
                 cnt_sc, h_ref, m_ref, route_ref):
    tm = x.shape[0]
    ys = [jnp.dot(pooled[:, g * POOL_GROUP_WIDTH:(g + 1) * POOL_GROUP_WIDTH].astype(BF16), wpool_ref[g],
                  preferred_element_type=F32) for g in range(len(POOL_WINDOWS))]
    pool = (jnp.concatenate(ys, axis=1) * spool_ref[...]).astype(BF16)
    mixed = jnp.concatenate([attn_bf, pool], axis=1)
    h = x + jnp.dot(mixed, wout_ref[...], preferred_element_type=F32)
    h_ref[...] = h
    m = _rms(h, gffn_ref[...])
    m_ref[...] = m

    logits = jnp.dot(m, wr_ref[...], precision=lax.Precision.HIGHEST, preferred_element_type=F32) + br_ref[...]
    lane = lax.broadcasted_iota(jnp.int32, (tm, LANES), 1)
    lanef = lane.astype(F32)
    neg = -jnp.inf
    big = 1e9

    def first_argmax(vals):
        top = jnp.max(vals, axis=-1, keepdims=True)
        idx = jnp.min(jnp.where(vals == top, lanef, big), axis=-1, keepdims=True)
        return top, idx

    gmask = lane < N_EXPERT_GROUPS
    gmax, gidx = first_argmax(jnp.where(gmask, logits, neg))
    gsum = jnp.sum(jnp.where(gmask, jnp.exp(jnp.where(gmask, logits - gmax, 0.0)), 0.0), axis=-1, keepdims=True)
    g_w = 1.0 / gsum
    lo = N_EXPERT_GROUPS + EXPERTS_PER_GROUP * gidx
    le = jnp.where((lanef >= lo) & (lanef < lo + EXPERTS_PER_GROUP), logits, neg)
    m1, i1 = first_argmax(le)
    m2, i2 = first_argmax(jnp.where(lanef == i1, neg, le))
    e2 = jnp.exp(m2 - m1)
    den = 1.0 + e2
    gate1 = g_w / den
    gate2 = g_w * e2 / den
    ex1 = i1 - N_EXPERT_GROUPS
    ex2 = i2 - N_EXPERT_GROUPS

    oh1 = lanef == ex1
    oh2 = lanef == ex2
    onehot = jnp.where(oh1, 1.0, 0.0) + jnp.where(oh2, 1.0, 0.0)
    before = jnp.dot(tri_ref[...], onehot.astype(BF16), preferred_element_type=F32) + cnt_sc[...]
    rank1 = jnp.sum(jnp.where(oh1, before, 0.0), axis=-1, keepdims=True)
    rank2 = jnp.sum(jnp.where(oh2, before, 0.0), axis=-1, keepdims=True)
    cnt_sc[...] += jnp.sum(onehot, axis=0, keepdims=True)

    route = jnp.zeros((tm, LANES), F32)
    for col, val in enumerate((ex1, ex2, rank1, rank2, gate1, gate2)):
        route = jnp.where(lane == col, val, route)
    route_ref[...] = route


def _post_prompt_kernel(x_ref, attn_ref, u_ref, uprev_ref, cnt_in_ref,
                        wpool_ref, spool_ref, wout_ref, gffn_ref, wr_ref, br_ref, tri_ref,
                        h_ref, m_ref, route_ref, cnt_ref, ext_ref, cnt_sc, *, tiles_per_seq):
    i = pl.program_id(0)
    tm = x_ref.shape[0]

    @pl.when(i == 0)
    def _():
        cnt_sc[...] = cnt_in_ref[...]

    seq_tile = i % tiles_per_seq
    u = u_ref[...]
    ext_ref[:HALO, :] = jnp.where(seq_tile == 0, 0.0, uprev_ref[...])
    ext_ref[HALO:, :] = u
    pos1 = seq_tile * tm + lax.broadcasted_iota(jnp.int32, (tm, POOL_GROUP_WIDTH), 0) + 1
    parts = []
    for g, w in enumerate(POOL_WINDOWS):
        sl = slice(g * POOL_GROUP_WIDTH, (g + 1) * POOL_GROUP_WIDTH)
        ws = u[:, sl]
        for j in range(1, w):
            ws = ws + ext_ref[HALO - j:HALO - j + tm, sl]
        cnt = jnp.minimum(pos1, w).astype(F32)
        parts.append(ws / cnt - u[:, sl])
    pooled = jnp.concatenate(parts, axis=1)
    _post_common(x_ref[...], attn_ref[...], pooled, wpool_ref, spool_ref, wout_ref, gffn_ref, wr_ref, br_ref,
                 tri_ref, cnt_sc, h_ref, m_ref, route_ref)
    cnt_ref[...] = cnt_sc[...]


def _post_sample_kernel(x_ref, attn_ref, ext3_ref, cnt_in_ref,
                        wpool_ref, spool_ref, wout_ref, gffn_ref, wr_ref, br_ref, tri_ref,
                        h_ref, m_ref, route_ref, cnt_ref, cnt_sc, *, past_len):
    cnt_sc[...] = cnt_in_ref[...]
    n_rows = ext3_ref.shape[1]
    parts = []
    for g, w in enumerate(POOL_WINDOWS):
        sl = slice(g * POOL_GROUP_WIDTH, (g + 1) * POOL_GROUP_WIDTH)
        win = ext3_ref[:, n_rows - w:, sl]
        parts.append(jnp.sum(win, axis=1) / float(min(past_len + 1, w)) - ext3_ref[:, n_rows - 1, sl])
    pooled = jnp.concatenate(parts, axis=1)
    _post_common(x_ref[...], attn_ref[...], pooled, wpool_ref, spool_ref, wout_ref, gffn_ref, wr_ref, br_ref,
                 tri_ref, cnt_sc, h_ref, m_ref, route_ref)
    cnt_ref[...] = cnt_sc[...]


def _post_weights_specs(weights, nargs):
    return [pl.BlockSpec(a.shape, (lambda *_, nd=a.ndim: (0,) * nd)) for a in weights]


def _strict_lower(tm):
    return jnp.asarray(np.tril(np.ones((tm, tm), np.float32), -1), BF16)


def _post_out(n, tm):
    row = lambda w: pl.BlockSpec((tm, w), lambda i: (i, 0))
    specs = [row(D_MODEL), row(D_MODEL), row(LANES), pl.BlockSpec((1, LANES), lambda i: (0, 0))]
    shapes = [jax.ShapeDtypeStruct((n, D_MODEL), F32), jax.ShapeDtypeStruct((n, D_MODEL), F32),
              jax.ShapeDtypeStruct((n, LANES), F32), jax.ShapeDtypeStruct((1, LANES), F32)]
    return specs, shapes


def _post_prompt(x, attn, u, cnt_in, weights, seq, tm):
    n = x.shape[0]
    row = lambda w: pl.BlockSpec((tm, w), lambda i: (i, 0))
    halo_blocks = tm // HALO
    tri = _strict_lower(tm)
    out_specs, out_shape = _post_out(n, tm)
    return pl.pallas_call(
        functools.partial(_post_prompt_kernel, tiles_per_seq=seq // tm),
        grid=(n // tm,),
        in_specs=[row(D_MODEL), row(ATTN_WIDTH), row(POOL_WIDTH),
                  pl.BlockSpec((HALO, POOL_WIDTH), lambda i: (jnp.maximum(i * halo_blocks - 1, 0), 0)),
                  pl.BlockSpec((1, LANES), lambda i: (0, 0))]
                 + _post_weights_specs(weights + (tri,), 1),
        out_specs=out_specs,
        out_shape=out_shape,
        scratch_shapes=[pltpu.VMEM((tm + HALO, POOL_WIDTH), F32), pltpu.VMEM((1, LANES), F32)],
        compiler_params=_cparams("arbitrary"),
        name="post_prompt",
    )(x, attn, u, u, cnt_in, *weights, tri)


def _post_sample(x, attn, ext3, cnt_in, weights, past_len):
    n = x.shape[0]
    tri = _strict_lower(n)
    out_specs, out_shape = _post_out(n, n)
    full = lambda a: pl.BlockSpec(a.shape, lambda i, nd=a.ndim: (0,) * nd)
    return pl.pallas_call(
        functools.partial(_post_sample_kernel, past_len=past_len),
        grid=(1,),
        in_specs=[full(x), full(attn), full(ext3), full(cnt_in)] + _post_weights_specs(weights + (tri,), 1),
        out_specs=out_specs,
        out_shape=out_shape,
        scratch_shapes=[pltpu.VMEM((1, LANES), F32)],
        compiler_params=_cparams("arbitrary"),
        name="post_sample",
    )(x, attn, ext3, cnt_in, *weights, tri)


DMA_WAIT_ROWS = 256


def _row_copy(src, src_row, dst, dst_row, sem):
    return pltpu.make_async_copy(src.at[pl.ds(src_row, 1)], dst.at[pl.ds(dst_row, 1)], sem)


def _wait_rows(src, dst, sem, n_rows):
    assert n_rows % DMA_WAIT_ROWS == 0 or n_rows < DMA_WAIT_ROWS
    chunk = min(n_rows, DMA_WAIT_ROWS)
    for _ in range(n_rows // chunk):
        pltpu.make_async_copy(src.at[pl.ds(0, chunk)], dst.at[pl.ds(0, chunk)], sem).wait()


def _dispatch_kernel(dest_ref, m_ref, xs_in_ref, xs_ref, sem, *, tokens_per_step):
    del xs_in_ref

    def body(t, carry):
        for k in range(2):
            _row_copy(m_ref, t, xs_ref, dest_ref[2 * t + k], sem).start()
        return carry

    lax.fori_loop(0, tokens_per_step, body, 0)
    _wait_rows(m_ref, xs_ref, sem, 2 * tokens_per_step)


def _dispatch(dest, m, xs, tokens_per_step):
    n = m.shape[0]
    dest_block = dest.shape[0] // (n // tokens_per_step)
    return pl.pallas_call(
        functools.partial(_dispatch_kernel, tokens_per_step=tokens_per_step),
        grid=(n // tokens_per_step,),
        in_specs=[pl.BlockSpec((dest_block,), lambda i: (i,), memory_space=pltpu.SMEM),
                  pl.BlockSpec((tokens_per_step, D_MODEL), lambda i: (i, 0)),
                  pl.BlockSpec(memory_space=pl.ANY)],
        out_specs=pl.BlockSpec(memory_space=pl.ANY),
        out_shape=jax.ShapeDtypeStruct(xs.shape, xs.dtype),
        scratch_shapes=[pltpu.SemaphoreType.DMA(())],
        input_output_aliases={2: 0},
        compiler_params=_cparams("arbitrary"),
        name="moe_dispatch",
    )(dest, m, xs)


def _experts_kernel(blk_e_ref, blk_first_ref, n_used_ref, xs_ref, wg_ref, wu_ref, wd_ref, ys_ref,
                    wg_bf, wu_bf, wd_bf):
    i = pl.program_id(0)

    @pl.when(i < n_used_ref[0])
    def _():
        @pl.when(blk_first_ref[i] == 1)
        def _():
            wg_bf[...] = wg_ref[...].astype(BF16)
            wu_bf[...] = wu_ref[...].astype(BF16)
            wd_bf[...] = wd_ref[...].astype(BF16)

        x = xs_ref[...].astype(BF16)
        gate = jnp.dot(x, wg_bf[...], preferred_element_type=F32)
        up = jnp.dot(x, wu_bf[...], preferred_element_type=F32)
        hid = gate * jax.nn.sigmoid(gate) * up
        ys_ref[...] = jnp.dot(hid.astype(BF16), wd_bf[...], preferred_element_type=F32)

    @pl.when(i >= n_used_ref[0])
    def _():
        ys_ref[...] = jnp.zeros_like(ys_ref)


def _experts(blk_e, blk_first, n_used, xs, w_gate, w_up, w_down):
    n_blk = xs.shape[0] // MOE_BLOCK
    rows = lambda i, be, bf, nu: (jnp.maximum(jnp.minimum(i, nu[0] - 1), 0), 0)
    out_rows = lambda i, be, bf, nu: (i, 0)
    wsel = lambda i, be, bf, nu: (be[i], 0, 0)
    grid_spec = pltpu.PrefetchScalarGridSpec(
        num_scalar_prefetch=3,
        grid=(n_blk,),
        in_specs=[pl.BlockSpec((MOE_BLOCK, D_MODEL), rows),
                  pl.BlockSpec((None, D_MODEL, EXPERT_HIDDEN), wsel),
                  pl.BlockSpec((None, D_MODEL, EXPERT_HIDDEN), wsel),
                  pl.BlockSpec((None, EXPERT_HIDDEN, D_MODEL), wsel)],
        out_specs=pl.BlockSpec((MOE_BLOCK, D_MODEL), out_rows),
        scratch_shapes=[pltpu.VMEM((D_MODEL, EXPERT_HIDDEN), BF16), pltpu.VMEM((D_MODEL, EXPERT_HIDDEN), BF16),
                        pltpu.VMEM((EXPERT_HIDDEN, D_MODEL), BF16)],
    )
    return pl.pallas_call(
        _experts_kernel,
        grid_spec=grid_spec,
        out_shape=jax.ShapeDtypeStruct(xs.shape, F32),
        compiler_params=_cparams("arbitrary"),
        name="moe_experts",
    )(blk_e, blk_first, n_used, xs, w_gate, w_up, w_down)


def _final_kernel(dest_ref, h_ref, route_ref, p_ref, ys_ref, gple_ref, wpg_ref, wpp_ref, o_ref,
                  ya_ref, yb_ref, sem):
    tm = h_ref.shape[0]

    def body(t, carry):
        _row_copy(ys_ref, dest_ref[2 * t], ya_ref, t, sem).start()
        _row_copy(ys_ref, dest_ref[2 * t + 1], yb_ref, t, sem).start()
        return carry

    lax.fori_loop(0, tm, body, 0)
    _wait_rows(ys_ref, ya_ref, sem, tm)
    _wait_rows(ys_ref, yb_ref, sem, tm)

    route = route_ref[...]
    lane = lax.broadcasted_iota(jnp.int32, route.shape, 1)
    g1 = jnp.sum(jnp.where(lane == 4, route, 0.0), axis=-1, keepdims=True)
    g2 = jnp.sum(jnp.where(lane == 5, route, 0.0), axis=-1, keepdims=True)
    h = h_ref[...] + (g1 * ya_ref[...] + g2 * yb_ref[...])
    n = _rms(h, gple_ref[...]).astype(BF16)
    gate = jax.nn.sigmoid(jnp.dot(n, wpg_ref[...], preferred_element_type=F32))
    proj = jnp.dot(p_ref[...].astype(BF16), wpp_ref[...], preferred_element_type=F32)
    o_ref[...] = h + gate * proj


def _final(dest, h, route, p, ys, gple, wpg_bf, wpp_bf, tm):
    n = h.shape[0]
    dest_block = dest.shape[0] // (n // tm)
    row = lambda w: pl.BlockSpec((tm, w), lambda i: (i, 0))
    full = lambda a: pl.BlockSpec(a.shape, lambda i, nd=a.ndim: (0,) * nd)
    return pl.pallas_call(
        _final_kernel,
        grid=(n // tm,),
        in_specs=[pl.BlockSpec((dest_block,), lambda i: (i,), memory_space=pltpu.SMEM),
                  row(D_MODEL), row(LANES), row(PLE_DIM),
                  pl.BlockSpec(memory_space=pl.ANY),
                  full(gple), full(wpg_bf), full(wpp_bf)],
        out_specs=row(D_MODEL),
        out_shape=jax.ShapeDtypeStruct((n, D_MODEL), F32),
        scratch_shapes=[pltpu.VMEM((tm, D_MODEL), F32), pltpu.VMEM((tm, D_MODEL), F32),
                        pltpu.SemaphoreType.DMA(())],
        compiler_params=_cparams("arbitrary"),
        name="moe_combine_ple",
    )(dest, h, route, p, ys, gple, wpg_bf, wpp_bf)


SMEM_BLOCK = 1024


def _pad_to(a, size):
    return jnp.concatenate([a, jnp.zeros((size - a.shape[0],), a.dtype)])


def kernel(x_prompt, x_sample, cache_k, cache_v, state_pool, page_table, p_prompt, p_sample, g_mix, w_in, g_q, g_k, b_sb, w_pool, s_pool, w_out, g_ffn, w_router_group, b_router_group, w_router_expert, b_router_expert, w_gate, w_up, w_down, g_ple, w_ple_gate, w_ple_proj):
    batch, seq, _ = x_prompt.shape
    dec_b, dec_t, _ = x_sample.shape
    assert dec_t == 1 and g_mix.shape[0] == 1
    n_pages = page_table.shape[1]
    past_len = n_pages * PAGE_SIZE
    n_p = batch * seq
    n_s = dec_b

    row = lambda a: a.reshape(1, -1).astype(F32)
    gmix = row(g_mix[0])
    win_bf = w_in[0].astype(BF16)
    gq = row(jnp.tile(g_q[0], N_HEADS) * (SB_SCALE * LOG2E))
    gk = row(jnp.tile(g_k[0], N_HEADS))
    bias2 = (b_sb[0] * LOG2E).astype(F32)
    seg = jnp.asarray(np.kron(np.eye(N_HEADS, dtype=np.float32), np.ones((HEAD_DIM, HEAD_DIM), np.float32)), BF16)
    w_r = jnp.zeros((D_MODEL, LANES), F32)
    w_r = w_r.at[:, :N_EXPERT_GROUPS].set(w_router_group[0])
    w_r = w_r.at[:, N_EXPERT_GROUPS:N_EXPERT_GROUPS + N_EXPERTS].set(w_router_expert[0])
    b_r = jnp.zeros((1, LANES), F32)
    b_r = b_r.at[0, :N_EXPERT_GROUPS].set(b_router_group[0])
    b_r = b_r.at[0, N_EXPERT_GROUPS:N_EXPERT_GROUPS + N_EXPERTS].set(b_router_expert[0])
    post_w = (w_pool[0].astype(BF16), row(s_pool[0]), w_out[0].astype(BF16), row(g_ffn[0]), w_r, b_r)
    gple = row(g_ple[0])
    wpg_bf = w_ple_gate[0].astype(BF16)
    wpp_bf = w_ple_proj[0].astype(BF16)

    xp = x_prompt.reshape(n_p, D_MODEL)
    attn_block = 256
    q_p, kt_p, ktb_p, vt_p, vb_p, u_p = _proj(xp, gmix, win_bf, gq, gk, seg, tm=512, seq=seq, key_block=attn_block)
    attn_p = _attention(bias2, q_p, ktb_p, vb_p, batch, seq, tq=2 * attn_block, tk=attn_block)
    cnt0 = jnp.zeros((1, LANES), F32)
    h_p, m_p, route_p, cnt_p = _post_prompt(xp, attn_p, u_p, cnt0, post_w, seq, tm=256)

    xs_ = x_sample.reshape(n_s, D_MODEL)
    q_s, k_s, v_s, u_s = _proj(xs_, gmix, win_bf, gq, gk, seg, tm=n_s)
    pages_per_step = 8
    q_lanes = jnp.broadcast_to(q_s.astype(F32)[:, :, None], (n_s, ATTN_WIDTH, PAGE_SIZE))
    bias_rows = jnp.broadcast_to(jnp.tile(bias2, pages_per_step)[:, None], (pages_per_step * N_HEADS, PAGE_SIZE))
    attn_s = _decode_attention(page_table, q_lanes, bias_rows,
                               jnp.transpose(cache_k[0], (0, 2, 3, 1)), jnp.transpose(cache_v[0], (0, 2, 3, 1)),
                               pages_per_step=pages_per_step)
    zero_row = jnp.zeros((n_s, HALO - POOL_STATE - 1, POOL_WIDTH), F32)
    ext3 = jnp.concatenate([zero_row, state_pool[0], u_s[:, None, :]], axis=1)
    h_s, m_s, route_s, cnt = _post_sample(xs_, attn_s.astype(BF16), ext3, cnt_p, post_w, past_len)

    counts = cnt[0, :N_EXPERTS].astype(jnp.int32)
    nblk_e = (counts + MOE_BLOCK - 1) // MOE_BLOCK
    bends = jnp.cumsum(nblk_e)
    bstarts = bends - nblk_e
    pstarts = bstarts * MOE_BLOCK
    n_assign = 2 * (n_p + n_s)
    n_blk = n_assign // MOE_BLOCK + N_EXPERTS + (1 if n_assign % MOE_BLOCK else 0)
    n_used = bends[-1:]
    blk = jnp.arange(n_blk, dtype=jnp.int32)
    last_e = jnp.max(jnp.where(nblk_e > 0, jnp.arange(N_EXPERTS, dtype=jnp.int32), 0))
    blk_e = jnp.minimum(jnp.sum(blk[:, None] >= bends[None, :], axis=1).astype(jnp.int32), last_e)
    blk_first = jnp.any((blk[:, None] == bstarts[None, :]) & (nblk_e[None, :] > 0), axis=1).astype(jnp.int32)

    def dests(route):
        e = route[:, 0:2].astype(jnp.int32)
        rank = route[:, 2:4].astype(jnp.int32)
        start = jnp.sum(jnp.where(e[:, :, None] == jnp.arange(N_EXPERTS)[None, None, :], pstarts[None, None, :], 0),
                        axis=-1)
        return (start + rank).reshape(-1)

    dest_p = dests(route_p)
    dest_s = _pad_to(dests(route_s), SMEM_BLOCK)

    xs = jnp.zeros((n_blk * MOE_BLOCK, D_MODEL), F32)
    xs = _dispatch(dest_p, m_p, xs, tokens_per_step=SMEM_BLOCK // 2)
    xs = _dispatch(dest_s, m_s, xs, tokens_per_step=n_s)
    ys = _experts(blk_e, blk_first, n_used.astype(jnp.int32), xs, w_gate[0], w_up[0], w_down[0])

    y_p = _final(dest_p, h_p, route_p, p_prompt[0].reshape(n_p, PLE_DIM), ys, gple, wpg_bf, wpp_bf,
                 tm=SMEM_BLOCK // 2)
    y_s = _final(dest_s, h_s, route_s, p_sample[0].reshape(n_s, PLE_DIM), ys, gple, wpg_bf, wpp_bf, tm=n_s)

    heads = lambda a, b: a.reshape(1, b, -1, N_HEADS, HEAD_DIM)
    heads_t = lambda a: jnp.transpose(a.reshape(batch, N_HEADS, HEAD_DIM, seq), (0, 3, 1, 2))[None]
    pool_prompt = u_p.reshape(batch, seq, POOL_WIDTH)[:, seq - POOL_STATE:][None]
    pool_sample = ext3[:, HALO - POOL_STATE:][None]
    return (y_p.reshape(batch, seq, D_MODEL), y_s.reshape(dec_b, dec_t, D_MODEL),
            heads_t(kt_p), heads_t(vt_p), pool_prompt,
            heads(k_s, dec_b), heads(v_s, dec_b), pool_sample)
```

```python
import functools
import math

import numpy as np
import jax
import jax.numpy as jnp
from jax import lax
from jax.experimental import pallas as pl
from jax.experimental.pallas import tpu as pltpu

F32 = jnp.float32
BF16 = jnp.bfloat16

D_MODEL = 1024
ATTN_WIDTH = 512
POOL_WIDTH = 512
HEAD_DIM = 64
N_HEADS = 8
POOL_WINDOWS = (2, 4, 8, 16)
POOL_GROUP_WIDTH = 128
POOL_STATE = 15
PAGE_SIZE = 128
N_EXPERT_GROUPS = 4
EXPERTS_PER_GROUP = 8
N_EXPERTS = 32
EXPERT_HIDDEN = 512
PLE_DIM = 256
RMS_EPS = 1e-6
SB_SCALE = 1.0 / math.sqrt(HEAD_DIM)
LOG2E = 1.4426950408889634

LANES = 128
HALO = 16
MOE_BLOCK = 256
VMEM_LIMIT = 56 * 1024 * 1024


def _cparams(*sem):
    return pltpu.CompilerParams(dimension_semantics=sem, vmem_limit_bytes=VMEM_LIMIT)


def _rms(x, g):
    ms = jnp.mean(x * x, axis=-1, keepdims=True)
    return x * lax.rsqrt(ms + RMS_EPS) * g


def _softplus2(z):
    return jnp.maximum(z, 0.0) + jnp.log2(1.0 + jnp.exp2(-jnp.abs(z)))


def _proj_kernel(x_ref, gmix_ref, win_ref, gq_ref, gk_ref, seg_ref, q_ref, *out_refs, key_block):
    xn = _rms(x_ref[...], gmix_ref[...]).astype(BF16)
    z = jnp.dot(xn, win_ref[...], preferred_element_type=F32)
    seg = seg_ref[...]

    def head_norm(t, g):
        t2 = t * t
        hi = t2.astype(BF16)
        lo = (t2 - hi.astype(F32)).astype(BF16)
        ss = jnp.dot(hi, seg, preferred_element_type=F32) + jnp.dot(lo, seg, preferred_element_type=F32)
        return t * lax.rsqrt(ss * (1.0 / HEAD_DIM) + RMS_EPS) * g

    q = head_norm(z[:, :ATTN_WIDTH], gq_ref[...])
    k = head_norm(z[:, ATTN_WIDTH:2 * ATTN_WIDTH], gk_ref[...])
    v = z[:, 2 * ATTN_WIDTH:3 * ATTN_WIDTH]
    q_ref[...] = q.astype(BF16)
    if key_block is None:
        k_ref, v_ref, u_ref = out_refs
        k_ref[...] = k
        v_ref[...] = v
    else:
        kt_ref, ktb_ref, vt_ref, vb_ref, u_ref = out_refs
        kt = k.T
        kt_ref[...] = kt
        ktb = kt.astype(BF16)
        for c in range(ktb_ref.shape[0]):
            ktb_ref[c] = ktb[:, c * key_block:(c + 1) * key_block]
        vt_ref[...] = v.T
        vb_ref[...] = v.astype(BF16)
    u_ref[...] = z[:, 3 * ATTN_WIDTH:]


def _proj(x, gmix, win_bf, gq, gk, seg, tm, seq=None, key_block=None):
    n = x.shape[0]
    row = lambda w: pl.BlockSpec((tm, w), lambda i: (i, 0))
    full = lambda a: pl.BlockSpec(a.shape, lambda i: (0,) * a.ndim)
    rows_f32 = jax.ShapeDtypeStruct((n, ATTN_WIDTH), F32)
    rows_bf16 = jax.ShapeDtypeStruct((n, ATTN_WIDTH), BF16)
    if key_block is None:
        out_specs = [row(ATTN_WIDTH)] * 3 + [row(POOL_WIDTH)]
        out_shape = [rows_bf16, rows_f32, rows_f32, jax.ShapeDtypeStruct((n, POOL_WIDTH), F32)]
    else:
        batch, tiles = n // seq, seq // tm
        transposed = pl.BlockSpec((None, ATTN_WIDTH, tm), lambda i: (i // tiles, 0, i % tiles))
        t_shape = jax.ShapeDtypeStruct((batch, ATTN_WIDTH, seq), F32)
        blocks = tm // key_block
        out_specs = [row(ATTN_WIDTH), transposed,
                     pl.BlockSpec((None, blocks, ATTN_WIDTH, key_block), lambda i: (i // tiles, i % tiles, 0, 0)),
                     transposed, row(ATTN_WIDTH), row(POOL_WIDTH)]
        out_shape = [rows_bf16, t_shape,
                     jax.ShapeDtypeStruct((batch, seq // key_block, ATTN_WIDTH, key_block), BF16),
                     t_shape, rows_bf16, jax.ShapeDtypeStruct((n, POOL_WIDTH), F32)]
    return pl.pallas_call(
        functools.partial(_proj_kernel, key_block=key_block),
        grid=(n // tm,),
        in_specs=[row(D_MODEL), full(gmix), full(win_bf), full(gq), full(gk), full(seg)],
        out_specs=out_specs,
        out_shape=out_shape,
        compiler_params=_cparams("arbitrary"),
        name="proj",
    )(x, gmix, win_bf, gq, gk, seg)


ROW_CHUNK = 256


def _attn_kernel(bias_ref, q_ref, k_ref, v_ref, ntri_ref, o_ref, qm_ref, argp_ref, a_ref, totp_ref, c_ref, acc_ref,
                 *, tq, tk):
    hp = pl.program_id(1)
    i = pl.program_id(2)
    diag_blocks = tq // tk
    lane = lax.broadcasted_iota(jnp.int32, (tq, LANES), 1)
    q2 = q_ref[...]
    zero = jnp.zeros_like(q2)
    qm_ref[0] = jnp.where(lane < HEAD_DIM, q2, zero)
    qm_ref[1] = jnp.where(lane >= HEAD_DIM, q2, zero)
    bs = (bias_ref[2 * hp], bias_ref[2 * hp + 1])
    sign = jnp.int32(-2 ** 31)
    chunks = [slice(r, r + ROW_CHUNK) for r in range(0, tq, ROW_CHUNK)]

    def causal(rs, d):
        row = lax.broadcasted_iota(jnp.int32, (ROW_CHUNK, tk), 0) + rs.start
        col = lax.broadcasted_iota(jnp.int32, (ROW_CHUNK, tk), 1) + d * tk
        return col < row

    def scores(ks, rs):
        return [jnp.dot(qm_ref[h, rs], ks, preferred_element_type=F32) + bs[h] for h in range(2)]

    def suffix_sums(zs, rs, mask):
        for h, z in enumerate(zs):
            neg_abs = lax.bitcast_convert_type(lax.bitcast_convert_type(z, jnp.int32) | sign, F32)
            sp = jnp.maximum(z, 0.0) + jnp.log2(1.0 + jnp.exp2(neg_abs))
            spm = sp if mask is None else jnp.where(mask, sp, 0.0)
            later = jnp.dot(spm.astype(BF16), ntri_ref[...], preferred_element_type=F32)
            argp_ref[h, rs] = (z - sp) + later
            totp_ref[h, rs] = jnp.broadcast_to(jnp.sum(spm, axis=-1, keepdims=True), (ROW_CHUNK, LANES))

    def weights(rs, mask, first):
        for h in range(2):
            arg = argp_ref[h, rs]
            if not first:
                arg = arg - jnp.concatenate([c_ref[h, rs]] * (tk // LANES), axis=1)
            a = jnp.exp2(arg)
            if mask is not None:
                a = jnp.where(mask, a, 0.0)
            a_ref[h, rs] = a.astype(BF16)

    def weigh_values(vs, rs, first):
        for h in range(2):
            pv = jnp.dot(a_ref[h, rs], vs, preferred_element_type=F32)
            if first:
                acc_ref[h, rs] = pv
                c_ref[h, rs] = totp_ref[h, rs]
            else:
                acc_ref[h, rs] += pv
                c_ref[h, rs] += totp_ref[h, rs]

    def values(kb):
        return v_ref[pl.ds(pl.multiple_of(kb * tk, tk), tk), :]

    for d in reversed(range(diag_blocks)):
        kb = i * diag_blocks + d
        first = d == diag_blocks - 1
        ks, vs = k_ref[kb], values(kb)
        for rs in chunks:
            if rs.start + ROW_CHUNK - 1 <= d * tk:
                if first:
                    acc_ref[:, rs] = jnp.zeros((2, ROW_CHUNK, LANES), F32)
                    c_ref[:, rs] = jnp.zeros((2, ROW_CHUNK, LANES), F32)
                continue
            mask = causal(rs, d)
            suffix_sums(scores(ks, rs), rs, mask)
            weights(rs, mask, first)
            weigh_values(vs, rs, first)

    n_full = i * diag_blocks

    @pl.when(n_full > 0)
    def _():
        ks0 = k_ref[n_full - 1]
        for rs in chunks:
            suffix_sums(scores(ks0, rs), rs, None)

        def body(j, carry):
            cur = n_full - 1 - j
            ks, vs = k_ref[cur - 1], values(cur)
            zs = []
            for rs in chunks:
                weights(rs, None, first=False)
                weigh_values(vs, rs, first=False)
                zs.append(scores(ks, rs))
            for rs, z in zip(chunks, zs):
                suffix_sums(z, rs, None)
            return carry

        lax.fori_loop(0, n_full - 1, body, 0)
        vs0 = values(0)
        for rs in chunks:
            weights(rs, None, first=False)
            weigh_values(vs0, rs, first=False)

    o_ref[...] = jnp.where(lane < HEAD_DIM, acc_ref[0], acc_ref[1]).astype(BF16)


def _attention(bias2, q, ktb, vb, batch, seq, tq, tk):
    n = q.shape[0]
    nq = seq // tq
    nk = seq // tk
    ntri = jnp.asarray(-np.tril(np.ones((tk, tk), np.float32), -1), BF16)
    grid_spec = pltpu.PrefetchScalarGridSpec(
        num_scalar_prefetch=1,
        grid=(batch, N_HEADS // 2, nq),
        in_specs=[
            pl.BlockSpec((tq, LANES), lambda b, hp, i, bias: (b * nq + i, hp)),
            pl.BlockSpec((None, nk, LANES, tk), lambda b, hp, i, bias: (b, 0, hp, 0)),
            pl.BlockSpec((seq, LANES), lambda b, hp, i, bias: (b, hp)),
            pl.BlockSpec(ntri.shape, lambda b, hp, i, bias: (0, 0)),
        ],
        out_specs=pl.BlockSpec((tq, LANES), lambda b, hp, i, bias: (b * nq + i, hp)),
        scratch_shapes=[pltpu.VMEM((2, tq, LANES), BF16), pltpu.VMEM((2, tq, tk), F32), pltpu.VMEM((2, tq, tk), BF16),
                        pltpu.VMEM((2, tq, LANES), F32), pltpu.VMEM((2, tq, LANES), F32),
                        pltpu.VMEM((2, tq, LANES), F32)],
    )
    return pl.pallas_call(
        functools.partial(_attn_kernel, tq=tq, tk=tk),
        grid_spec=grid_spec,
        out_shape=jax.ShapeDtypeStruct((n, ATTN_WIDTH), BF16),
        compiler_params=_cparams("arbitrary", "arbitrary", "arbitrary"),
        name="attn_prompt",
    )(bias2, q, ktb, vb, ntri)


def _decode_kernel(pt_ref, q_ref, bias_ref, tri_ref, *rest, pages_per_step):
    k_refs = rest[:pages_per_step]
    v_refs = rest[pages_per_step:2 * pages_per_step]
    o_ref, c_ref, acc_ref = rest[2 * pages_per_step:]
    step = pl.program_id(1)

    @pl.when(step == 0)
    def _():
        c_ref[...] = jnp.zeros_like(c_ref)
        acc_ref[...] = jnp.zeros_like(acc_ref)

    head_row = lax.broadcasted_iota(jnp.int32, (N_HEADS, PAGE_SIZE), 0)
    zs = []
    for j in range(pages_per_step):
        zj = jnp.zeros((N_HEADS, PAGE_SIZE), F32)
        for h in range(N_HEADS):
            dims = slice(h * HEAD_DIM, (h + 1) * HEAD_DIM)
            zrow = jnp.sum(q_ref[dims, :] * k_refs[j][h], axis=0, keepdims=True)
            zj = jnp.where(head_row == h, zrow, zj)
        zs.append(zj)
    z = jnp.concatenate(zs, axis=0) + bias_ref[...]
    lk = -_softplus2(z)
    hi = lk.astype(BF16)
    lo = (lk - hi.astype(F32)).astype(BF16)
    r = (jnp.dot(hi, tri_ref[...], preferred_element_type=F32)
         + jnp.dot(lo, tri_ref[...], preferred_element_type=F32))
    tot = r[:, PAGE_SIZE:]
    c = c_ref[...]
    cs = []
    for j in range(pages_per_step):
        cs.append(c)
        c = c + tot[j * N_HEADS:(j + 1) * N_HEADS]
    c_ref[...] = c
    a = jnp.exp2(z + lk + r[:, :PAGE_SIZE] + jnp.concatenate(cs, axis=0))

    for h in range(N_HEADS):
        dims = slice(h * HEAD_DIM, (h + 1) * HEAD_DIM)
        acc = acc_ref[dims, :]
        for j in range(pages_per_step):
            row = j * N_HEADS + h
            acc = acc + a[row:row + 1, :] * v_refs[j][h]
        acc_ref[dims, :] = acc

    @pl.when(step == pl.num_programs(1) - 1)
    def _():
        o_ref[...] = jnp.sum(acc_ref[...], axis=1, keepdims=True)


def _decode_attention(page_table, q_lanes, bias_rows, cache_kt, cache_vt, pages_per_step):
    dec_b, n_pages = page_table.shape
    n_steps = n_pages // pages_per_step
    tri = np.zeros((PAGE_SIZE, 2 * PAGE_SIZE), np.float32)
    tri[:, :PAGE_SIZE] = np.tril(np.ones((PAGE_SIZE, PAGE_SIZE), np.float32), -1)
    tri[:, PAGE_SIZE:] = 1.0
    tri = jnp.asarray(tri, BF16)

    def page_spec(j):
        def index(b, s, pt):
            page = n_pages - 1 - (s * pages_per_step + j)
            return (pt[b * n_pages + page], 0, 0, 0)
        return pl.BlockSpec((None, N_HEADS, HEAD_DIM, PAGE_SIZE), index)

    const = lambda a: pl.BlockSpec(a.shape, lambda b, s, pt: (0,) * a.ndim)
    grid_spec = pltpu.PrefetchScalarGridSpec(
        num_scalar_prefetch=1,
        grid=(dec_b, n_steps),
        in_specs=[pl.BlockSpec((None, ATTN_WIDTH, PAGE_SIZE), lambda b, s, pt: (b, 0, 0)),
                  const(bias_rows), const(tri)]
                 + [page_spec(j) for j in range(pages_per_step)] * 2,
        out_specs=pl.BlockSpec((None, ATTN_WIDTH, 1), lambda b, s, pt: (b, 0, 0)),
        scratch_shapes=[pltpu.VMEM((N_HEADS, PAGE_SIZE), F32), pltpu.VMEM((ATTN_WIDTH, PAGE_SIZE), F32)],
    )
    out = pl.pallas_call(
        functools.partial(_decode_kernel, pages_per_step=pages_per_step),
        grid_spec=grid_spec,
        out_shape=jax.ShapeDtypeStruct((dec_b, ATTN_WIDTH, 1), F32),
        compiler_params=_cparams("arbitrary", "arbitrary"),
        name="attn_decode",
    )(page_table.reshape(-1), q_lanes, bias_rows, tri,
      *([cache_kt] * pages_per_step), *([cache_vt] * pages_per_step))
    return out.reshape(dec_b, ATTN_WIDTH)


def _post_common(x, attn_bf, pooled, wpool_ref, spool_ref, wout_ref, gffn_ref, wr_ref, br_ref, tri_ref,
                 cnt_sc, h_ref, m_ref, route_ref):
    tm = x.shape[0]
    ys = [jnp.dot(pooled[:, g * POOL_GROUP_WIDTH:(g + 1) * POOL_GROUP_WIDTH].astype(BF16), wpool_ref[g],
                  preferred_element_type=F32) for g in range(len(POOL_WINDOWS))]
    pool = (jnp.concatenate(ys, axis=1) * spool_ref[...]).astype(BF16)
    mixed = jnp.concatenate([attn_bf, pool], axis=1)
    h = x + jnp.dot(mixed, wout_ref[...], preferred_element_type=F32)
    h_ref[...] = h
    m = _rms(h, gffn_ref[...])
    m_ref[...] = m

    m_hi = m.astype(BF16)
    m_lo = (m - m_hi.astype(F32)).astype(BF16)
    wr = wr_ref[...]
    r_hi = jnp.dot(m_hi, wr, preferred_element_type=F32)
    r_lo = jnp.dot(m_lo, wr[:, :LANES], preferred_element_type=F32)
    logits = r_hi[:, :LANES] + r_hi[:, LANES:] + r_lo + br_ref[...]
    lane = lax.broadcasted_iota(jnp.int32, (tm, LANES), 1)
    lanef = lane.astype(F32)
    neg = -jnp.inf
    big = 1e9

    def first_argmax(vals):
        top = jnp.max(vals, axis=-1, keepdims=True)
        idx = jnp.min(jnp.where(vals == top, lanef, big), axis=-1, keepdims=True)
        return top, idx

    gmask = lane < N_EXPERT_GROUPS
    gmax, gidx = first_argmax(jnp.where(gmask, logits, neg))
    gsum = jnp.sum(jnp.where(gmask, jnp.exp(jnp.where(gmask, logits - gmax, 0.0)), 0.0), axis=-1, keepdims=True)
    g_w = 1.0 / gsum
    lo = N_EXPERT_GROUPS + EXPERTS_PER_GROUP * gidx
    le = jnp.where((lanef >= lo) & (lanef < lo + EXPERTS_PER_GROUP), logits, neg)
    m1, i1 = first_argmax(le)
    m2, i2 = first_argmax(jnp.where(lanef == i1, neg, le))
    e2 = jnp.exp(m2 - m1)
    den = 1.0 + e2
    gate1 = g_w / den
    gate2 = g_w * e2 / den
    ex1 = i1 - N_EXPERT_GROUPS
    ex2 = i2 - N_EXPERT_GROUPS

    oh1 = lanef == ex1
    oh2 = lanef == ex2
    onehot = jnp.where(oh1, 1.0, 0.0) + jnp.where(oh2, 1.0, 0.0)
    before = jnp.dot(tri_ref[...], onehot.astype(BF16), preferred_element_type=F32) + cnt_sc[...]
    rank1 = jnp.sum(jnp.where(oh1, before, 0.0), axis=-1, keepdims=True)
    rank2 = jnp.sum(jnp.where(oh2, before, 0.0), axis=-1, keepdims=True)
    cnt_sc[...] += jnp.sum(onehot, axis=0, keepdims=True)

    route = jnp.zeros((tm, LANES), F32)
    for col, val in enumerate((ex1, ex2, rank1, rank2, gate1, gate2)):
        route = jnp.where(lane == col, val, route)
    route_ref[...] = route


def _post_prompt_kernel(x_ref, attn_ref, u_ref, uprev_ref, cnt_in_ref,
                        wpool_ref, spool_ref, wout_ref, gffn_ref, wr_ref, br_ref, tri_ref,
                        h_ref, m_ref, route_ref, cnt_ref, ext_ref, cnt_sc, *, tiles_per_seq):
    i = pl.program_id(0)
    tm = x_ref.shape[0]

    @pl.when(i == 0)
    def _():
        cnt_sc[...] = cnt_in_ref[...]

    seq_tile = i % tiles_per_seq
    u = u_ref[...]
    ext_ref[:HALO, :] = jnp.where(seq_tile == 0, 0.0, uprev_ref[...])
    ext_ref[HALO:, :] = u
    pos1 = seq_tile * tm + lax.broadcasted_iota(jnp.int32, (tm, POOL_GROUP_WIDTH), 0) + 1
    parts = []
    for g, w in enumerate(POOL_WINDOWS):
        sl = slice(g * POOL_GROUP_WIDTH, (g + 1) * POOL_GROUP_WIDTH)
        ws = u[:, sl]
        for j in range(1, w):
            ws = ws + ext_ref[HALO - j:HALO - j + tm, sl]
        cnt = jnp.minimum(pos1, w).astype(F32)
        parts.append(ws / cnt - u[:, sl])
    pooled = jnp.concatenate(parts, axis=1)
    _post_common(x_ref[...], attn_ref[...], pooled, wpool_ref, spool_ref, wout_ref, gffn_ref, wr_ref, br_ref,
                 tri_ref, cnt_sc, h_ref, m_ref, route_ref)
    cnt_ref[...] = cnt_sc[...]


def _post_sample_kernel(x_ref, attn_ref, ext3_ref, cnt_in_ref,
                        wpool_ref, spool_ref, wout_ref, gffn_ref, wr_ref, br_ref, tri_ref,
                        h_ref, m_ref, route_ref, cnt_ref, cnt_sc, *, past_len):
    cnt_sc[...] = cnt_in_ref[...]
    n_rows = ext3_ref.shape[1]
    parts = []
    for g, w in enumerate(POOL_WINDOWS):
        sl = slice(g * POOL_GROUP_WIDTH, (g + 1) * POOL_GROUP_WIDTH)
        win = ext3_ref[:, n_rows - w:, sl]
        parts.append(jnp.sum(win, axis=1) / float(min(past_len + 1, w)) - ext3_ref[:, n_rows - 1, sl])
    pooled = jnp.concatenate(parts, axis=1)
    _post_common(x_ref[...], attn_ref[...], pooled, wpool_ref, spool_ref, wout_ref, gffn_ref, wr_ref, br_ref,
                 tri_ref, cnt_sc, h_ref, m_ref, route_ref)
    cnt_ref[...] = cnt_sc[...]


def _post_weights_specs(weights, nargs):
    return [pl.BlockSpec(a.shape, (lambda *_, nd=a.ndim: (0,) * nd)) for a in weights]


def _strict_lower(tm):
    return jnp.asarray(np.tril(np.ones((tm, tm), np.float32), -1), BF16)


def _post_out(n, tm):
    row = lambda w: pl.BlockSpec((tm, w), lambda i: (i, 0))
    specs = [row(D_MODEL), row(D_MODEL), row(LANES), pl.BlockSpec((1, LANES), lambda i: (0, 0))]
    shapes = [jax.ShapeDtypeStruct((n, D_MODEL), F32), jax.ShapeDtypeStruct((n, D_MODEL), F32),
              jax.ShapeDtypeStruct((n, LANES), F32), jax.ShapeDtypeStruct((1, LANES), F32)]
    return specs, shapes


def _post_prompt(x, attn, u, cnt_in, weights, seq, tm):
    n = x.shape[0]
    row = lambda w: pl.BlockSpec((tm, w), lambda i: (i, 0))
    halo_blocks = tm // HALO
    tri = _strict_lower(tm)
    out_specs, out_shape = _post_out(n, tm)
    return pl.pallas_call(
        functools.partial(_post_prompt_kernel, tiles_per_seq=seq // tm),
        grid=(n // tm,),
        in_specs=[row(D_MODEL), row(ATTN_WIDTH), row(POOL_WIDTH),
                  pl.BlockSpec((HALO, POOL_WIDTH), lambda i: (jnp.maximum(i * halo_blocks - 1, 0), 0)),
                  pl.BlockSpec((1, LANES), lambda i: (0, 0))]
                 + _post_weights_specs(weights + (tri,), 1),
        out_specs=out_specs,
        out_shape=out_shape,
        scratch_shapes=[pltpu.VMEM((tm + HALO, POOL_WIDTH), F32), pltpu.VMEM((1, LANES), F32)],
        compiler_params=_cparams("arbitrary"),
        name="post_prompt",
    )(x, attn, u, u, cnt_in, *weights, tri)


def _post_sample(x, attn, ext3, cnt_in, weights, past_len):
    n = x.shape[0]
    tri = _strict_lower(n)
    out_specs, out_shape = _post_out(n, n)
    full = lambda a: pl.BlockSpec(a.shape, lambda i, nd=a.ndim: (0,) * nd)
    return pl.pallas_call(
        functools.partial(_post_sample_kernel, past_len=past_len),
        grid=(1,),
        in_specs=[full(x), full(attn), full(ext3), full(cnt_in)] + _post_weights_specs(weights + (tri,), 1),
        out_specs=out_specs,
        out_shape=out_shape,
        scratch_shapes=[pltpu.VMEM((1, LANES), F32)],
        compiler_params=_cparams("arbitrary"),
        name="post_sample",
    )(x, attn, ext3, cnt_in, *weights, tri)


DMA_WAIT_ROWS = 256


def _row_copy(src, src_row, dst, dst_row, sem):
    return pltpu.make_async_copy(src.at[pl.ds(src_row, 1)], dst.at[pl.ds(dst_row, 1)], sem)


def _wait_rows(src, dst, sem, n_rows):
    assert n_rows % DMA_WAIT_ROWS == 0 or n_rows < DMA_WAIT_ROWS
    chunk = min(n_rows, DMA_WAIT_ROWS)
    for _ in range(n_rows // chunk):
        pltpu.make_async_copy(src.at[pl.ds(0, chunk)], dst.at[pl.ds(0, chunk)], sem).wait()


def _dispatch_kernel(dest_ref, m_ref, xs_in_ref, xs_ref, sem, *, tokens_per_step):
    del xs_in_ref

    def body(t, carry):
        for k in range(2):
            _row_copy(m_ref, t, xs_ref, dest_ref[2 * t + k], sem).start(priority=k)
        return carry

    lax.fori_loop(0, tokens_per_step, body, 0)
    _wait_rows(m_ref, xs_ref, sem, 2 * tokens_per_step)


def _dispatch(dest, m, xs, tokens_per_step):
    n = m.shape[0]
    dest_block = dest.shape[0] // (n // tokens_per_step)
    return pl.pallas_call(
        functools.partial(_dispatch_kernel, tokens_per_step=tokens_per_step),
        grid=(n // tokens_per_step,),
        in_specs=[pl.BlockSpec((dest_block,), lambda i: (i,), memory_space=pltpu.SMEM),
                  pl.BlockSpec((tokens_per_step, D_MODEL), lambda i: (i, 0)),
                  pl.BlockSpec(memory_space=pl.ANY)],
        out_specs=pl.BlockSpec(memory_space=pl.ANY),
        out_shape=jax.ShapeDtypeStruct(xs.shape, xs.dtype),
        scratch_shapes=[pltpu.SemaphoreType.DMA(())],
        input_output_aliases={2: 0},
        compiler_params=_cparams("arbitrary"),
        name="moe_dispatch",
    )(dest, m, xs)


def _experts_kernel(blk_e_ref, blk_first_ref, n_used_ref, xs_ref, wg_ref, wu_ref, wd_ref, ys_ref,
                    wg_bf, wu_bf, wd_bf):
    i = pl.program_id(0)

    @pl.when(i < n_used_ref[0])
    def _():
        @pl.when(blk_first_ref[i] == 1)
        def _():
            wg_bf[...] = wg_ref[...].astype(BF16)
            wu_bf[...] = wu_ref[...].astype(BF16)
            wd_bf[...] = wd_ref[...].astype(BF16)

        x = xs_ref[...].astype(BF16)
        gate = jnp.dot(x, wg_bf[...], preferred_element_type=F32)
        up = jnp.dot(x, wu_bf[...], preferred_element_type=F32)
        hid = gate * jax.nn.sigmoid(gate) * up
        ys_ref[...] = jnp.dot(hid.astype(BF16), wd_bf[...], preferred_element_type=F32)

    @pl.when(i >= n_used_ref[0])
    def _():
        ys_ref[...] = jnp.zeros_like(ys_ref)


def _experts(blk_e, blk_first, n_used, xs, w_gate, w_up, w_down):
    n_blk = xs.shape[0] // MOE_BLOCK
    rows = lambda i, be, bf, nu: (jnp.maximum(jnp.minimum(i, nu[0] - 1), 0), 0)
    out_rows = lambda i, be, bf, nu: (i, 0)
    wsel = lambda i, be, bf, nu: (be[i], 0, 0)
    grid_spec = pltpu.PrefetchScalarGridSpec(
        num_scalar_prefetch=3,
        grid=(n_blk,),
        in_specs=[pl.BlockSpec((MOE_BLOCK, D_MODEL), rows),
                  pl.BlockSpec((None, D_MODEL, EXPERT_HIDDEN), wsel),
                  pl.BlockSpec((None, D_MODEL, EXPERT_HIDDEN), wsel),
                  pl.BlockSpec((None, EXPERT_HIDDEN, D_MODEL), wsel)],
        out_specs=pl.BlockSpec((MOE_BLOCK, D_MODEL), out_rows),
        scratch_shapes=[pltpu.VMEM((D_MODEL, EXPERT_HIDDEN), BF16), pltpu.VMEM((D_MODEL, EXPERT_HIDDEN), BF16),
                        pltpu.VMEM((EXPERT_HIDDEN, D_MODEL), BF16)],
    )
    return pl.pallas_call(
        _experts_kernel,
        grid_spec=grid_spec,
        out_shape=jax.ShapeDtypeStruct(xs.shape, F32),
        compiler_params=_cparams("arbitrary"),
        name="moe_experts",
    )(blk_e, blk_first, n_used, xs, w_gate, w_up, w_down)


def _final_kernel(dest_ref, h_ref, route_ref, p_ref, ys_ref, gple_ref, wpg_ref, wpp_ref, o_ref,
                  ya_ref, yb_ref, sem):
    tm = h_ref.shape[0]

    def body(t, carry):
        _row_copy(ys_ref, dest_ref[2 * t], ya_ref, t, sem).start(priority=0)
        _row_copy(ys_ref, dest_ref[2 * t + 1], yb_ref, t, sem).start(priority=1)
        return carry

    lax.fori_loop(0, tm, body, 0)
    _wait_rows(ys_ref, ya_ref, sem, tm)
    _wait_rows(ys_ref, yb_ref, sem, tm)

    route = route_ref[...]
    lane = lax.broadcasted_iota(jnp.int32, route.shape, 1)
    g1 = jnp.sum(jnp.where(lane == 4, route, 0.0), axis=-1, keepdims=True)
    g2 = jnp.sum(jnp.where(lane == 5, route, 0.0), axis=-1, keepdims=True)
    h = h_ref[...] + (g1 * ya_ref[...] + g2 * yb_ref[...])
    n = _rms(h, gple_ref[...]).astype(BF16)
    gate = jax.nn.sigmoid(jnp.dot(n, wpg_ref[...], preferred_element_type=F32))
    proj = jnp.dot(p_ref[...].astype(BF16), wpp_ref[...], preferred_element_type=F32)
    o_ref[...] = h + gate * proj


def _final(dest, h, route, p, ys, gple, wpg_bf, wpp_bf, tm):
    n = h.shape[0]
    dest_block = dest.shape[0] // (n // tm)
    row = lambda w: pl.BlockSpec((tm, w), lambda i: (i, 0))
    full = lambda a: pl.BlockSpec(a.shape, lambda i, nd=a.ndim: (0,) * nd)
    return pl.pallas_call(
        _final_kernel,
        grid=(n // tm,),
        in_specs=[pl.BlockSpec((dest_block,), lambda i: (i,), memory_space=pltpu.SMEM),
                  row(D_MODEL), row(LANES), row(PLE_DIM),
                  pl.BlockSpec(memory_space=pl.ANY),
                  full(gple), full(wpg_bf), full(wpp_bf)],
        out_specs=row(D_MODEL),
        out_shape=jax.ShapeDtypeStruct((n, D_MODEL), F32),
        scratch_shapes=[pltpu.VMEM((tm, D_MODEL), F32), pltpu.VMEM((tm, D_MODEL), F32),
                        pltpu.SemaphoreType.DMA(())],
        compiler_params=_cparams("arbitrary"),
        name="moe_combine_ple",
    )(dest, h, route, p, ys, gple, wpg_bf, wpp_bf)


SMEM_BLOCK = 1024


def _pad_to(a, size):
    return jnp.concatenate([a, jnp.zeros((size - a.shape[0],), a.dtype)])


def kernel(x_prompt, x_sample, cache_k, cache_v, state_pool, page_table, p_prompt, p_sample, g_mix, w_in, g_q, g_k, b_sb, w_pool, s_pool, w_out, g_ffn, w_router_group, b_router_group, w_router_expert, b_router_expert, w_gate, w_up, w_down, g_ple, w_ple_gate, w_ple_proj):
    batch, seq, _ = x_prompt.shape
    dec_b, dec_t, _ = x_sample.shape
    assert dec_t == 1 and g_mix.shape[0] == 1
    n_pages = page_table.shape[1]
    past_len = n_pages * PAGE_SIZE
    n_p = batch * seq
    n_s = dec_b

    row = lambda a: a.reshape(1, -1).astype(F32)
    gmix = row(g_mix[0])
    win_bf = w_in[0].astype(BF16)
    gq = row(jnp.tile(g_q[0], N_HEADS) * (SB_SCALE * LOG2E))
    gk = row(jnp.tile(g_k[0], N_HEADS))
    bias2 = (b_sb[0] * LOG2E).astype(F32)
    seg = jnp.asarray(np.kron(np.eye(N_HEADS, dtype=np.float32), np.ones((HEAD_DIM, HEAD_DIM), np.float32)), BF16)
    w_r = jnp.zeros((D_MODEL, LANES), F32)
    w_r = w_r.at[:, :N_EXPERT_GROUPS].set(w_router_group[0])
    w_r = w_r.at[:, N_EXPERT_GROUPS:N_EXPERT_GROUPS + N_EXPERTS].set(w_router_expert[0])
    b_r = jnp.zeros((1, LANES), F32)
    b_r = b_r.at[0, :N_EXPERT_GROUPS].set(b_router_group[0])
    b_r = b_r.at[0, N_EXPERT_GROUPS:N_EXPERT_GROUPS + N_EXPERTS].set(b_router_expert[0])
    w_r_hi = w_r.astype(BF16)
    w_r_split = jnp.concatenate([w_r_hi, (w_r - w_r_hi.astype(F32)).astype(BF16)], axis=1)
    post_w = (w_pool[0].astype(BF16), row(s_pool[0]), w_out[0].astype(BF16), row(g_ffn[0]), w_r_split, b_r)
    gple = row(g_ple[0])
    wpg_bf = w_ple_gate[0].astype(BF16)
    wpp_bf = w_ple_proj[0].astype(BF16)

    xp = x_prompt.reshape(n_p, D_MODEL)
    attn_block = 256
    q_p, kt_p, ktb_p, vt_p, vb_p, u_p = _proj(xp, gmix, win_bf, gq, gk, seg, tm=512, seq=seq, key_block=attn_block)
    attn_p = _attention(bias2, q_p, ktb_p, vb_p, batch, seq, tq=2 * attn_block, tk=attn_block)
    cnt0 = jnp.zeros((1, LANES), F32)
    h_p, m_p, route_p, cnt_p = _post_prompt(xp, attn_p, u_p, cnt0, post_w, seq, tm=256)

    xs_ = x_sample.reshape(n_s, D_MODEL)
    q_s, k_s, v_s, u_s = _proj(xs_, gmix, win_bf, gq, gk, seg, tm=n_s)
    pages_per_step = 16
    q_lanes = jnp.broadcast_to(q_s.astype(F32)[:, :, None], (n_s, ATTN_WIDTH, PAGE_SIZE))
    bias_rows = jnp.broadcast_to(jnp.tile(bias2, pages_per_step)[:, None], (pages_per_step * N_HEADS, PAGE_SIZE))
    attn_s = _decode_attention(page_table, q_lanes, bias_rows,
                               jnp.transpose(cache_k[0], (0, 2, 3, 1)), jnp.transpose(cache_v[0], (0, 2, 3, 1)),
                               pages_per_step=pages_per_step)
    zero_row = jnp.zeros((n_s, HALO - POOL_STATE - 1, POOL_WIDTH), F32)
    ext3 = jnp.concatenate([zero_row, state_pool[0], u_s[:, None, :]], axis=1)
    h_s, m_s, route_s, cnt = _post_sample(xs_, attn_s.astype(BF16), ext3, cnt_p, post_w, past_len)

    counts = cnt[0, :N_EXPERTS].astype(jnp.int32)
    nblk_e = (counts + MOE_BLOCK - 1) // MOE_BLOCK
    bends = jnp.cumsum(nblk_e)
    bstarts = bends - nblk_e
    pstarts = bstarts * MOE_BLOCK
    n_assign = 2 * (n_p + n_s)
    n_blk = n_assign // MOE_BLOCK + N_EXPERTS + (1 if n_assign % MOE_BLOCK else 0)
    n_used = bends[-1:]
    blk = jnp.arange(n_blk, dtype=jnp.int32)
    last_e = jnp.max(jnp.where(nblk_e > 0, jnp.arange(N_EXPERTS, dtype=jnp.int32), 0))
    blk_e = jnp.minimum(jnp.sum(blk[:, None] >= bends[None, :], axis=1).astype(jnp.int32), last_e)
    blk_first = jnp.any((blk[:, None] == bstarts[None, :]) & (nblk_e[None, :] > 0), axis=1).astype(jnp.int32)

    def dests(route):
        e = route[:, 0:2].astype(jnp.int32)
        rank = route[:, 2:4].astype(jnp.int32)
        start = jnp.sum(jnp.where(e[:, :, None] == jnp.arange(N_EXPERTS)[None, None, :], pstarts[None, None, :], 0),
                        axis=-1)
        return (start + rank).reshape(-1)

    dest_p = dests(route_p)
    dest_s = _pad_to(dests(route_s), SMEM_BLOCK)

    xs = jnp.zeros((n_blk * MOE_BLOCK, D_MODEL), F32)
    xs = _dispatch(dest_p, m_p, xs, tokens_per_step=SMEM_BLOCK // 2)
    xs = _dispatch(dest_s, m_s, xs, tokens_per_step=n_s)
    ys = _experts(blk_e, blk_first, n_used.astype(jnp.int32), xs, w_gate[0], w_up[0], w_down[0])

    y_p = _final(dest_p, h_p, route_p, p_prompt[0].reshape(n_p, PLE_DIM), ys, gple, wpg_bf, wpp_bf,
                 tm=SMEM_BLOCK // 2)
    y_s = _final(dest_s, h_s, route_s, p_sample[0].reshape(n_s, PLE_DIM), ys, gple, wpg_bf, wpp_bf, tm=n_s)

    heads = lambda a, b: a.reshape(1, b, -1, N_HEADS, HEAD_DIM)
    heads_t = lambda a: jnp.transpose(a.reshape(batch, N_HEADS, HEAD_DIM, seq), (0, 3, 1, 2))[None]
    pool_prompt = u_p.reshape(batch, seq, POOL_WIDTH)[:, seq - POOL_STATE:][None]
    pool_sample = ext3[:, HALO - POOL_STATE:][None]
    return (y_p.reshape(batch, seq, D_MODEL), y_s.reshape(dec_b, dec_t, D_MODEL),
            heads_t(kt_p), heads_t(vt_p), pool_prompt,
            heads(k_s, dec_b), heads(v_s, dec_b), pool_sample)
```

```python
import functools
import math

import numpy as np
import jax
import jax.numpy as jnp
from jax import lax
from jax.experimental import pallas as pl
from jax.experimental.pallas import tpu as pltpu

F32 = jnp.float32
BF16 = jnp.bfloat16

D_MODEL = 1024
ATTN_WIDTH = 512
POOL_WIDTH = 512
HEAD_DIM = 64
N_HEADS = 8
POOL_WINDOWS = (2, 4, 8, 16)
POOL_GROUP_WIDTH = 128
POOL_STATE = 15
PAGE_SIZE = 128
N_EXPERT_GROUPS = 4
EXPERTS_PER_GROUP = 8
N_EXPERTS = 32
EXPERT_HIDDEN = 512
PLE_DIM = 256
RMS_EPS = 1e-6
SB_SCALE = 1.0 / math.sqrt(HEAD_DIM)
LOG2E = 1.4426950408889634

LANES = 128
HALO = 16
MOE_BLOCK = 256
VMEM_LIMIT = 56 * 1024 * 1024


def _cparams(*sem):
    return pltpu.CompilerParams(dimension_semantics=sem, vmem_limit_bytes=VMEM_LIMIT)


def _rms(x, g):
    ms = jnp.mean(x * x, axis=-1, keepdims=True)
    return x * lax.rsqrt(ms + RMS_EPS) * g


def _softplus2(z):
    return jnp.maximum(z, 0.0) + jnp.log2(1.0 + jnp.exp2(-jnp.abs(z)))


def _proj_kernel(x_ref, gmix_ref, win_ref, gq_ref, gk_ref, seg_ref, q_ref, *out_refs, key_block):
    xn = _rms(x_ref[...], gmix_ref[...]).astype(BF16)
    z = jnp.dot(xn, win_ref[...], preferred_element_type=F32)
    seg = seg_ref[...]

    def head_norm(t, g):
        t2 = t * t
        hi = t2.astype(BF16)
        lo = (t2 - hi.astype(F32)).astype(BF16)
        ss = jnp.dot(hi, seg, preferred_element_type=F32) + jnp.dot(lo, seg, preferred_element_type=F32)
        return t * lax.rsqrt(ss * (1.0 / HEAD_DIM) + RMS_EPS) * g

    q = head_norm(z[:, :ATTN_WIDTH], gq_ref[...])
    k = head_norm(z[:, ATTN_WIDTH:2 * ATTN_WIDTH], gk_ref[...])
    v = z[:, 2 * ATTN_WIDTH:3 * ATTN_WIDTH]
    q_ref[...] = q.astype(BF16)
    if key_block is None:
        k_ref, v_ref, u_ref = out_refs
        k_ref[...] = k
        v_ref[...] = v
    else:
        kt_ref, ktb_ref, vt_ref, vb_ref, u_ref = out_refs
        kt = k.T
        kt_ref[...] = kt
        ktb = kt.astype(BF16)
        for c in range(ktb_ref.shape[0]):
            ktb_ref[c] = ktb[:, c * key_block:(c + 1) * key_block]
        vt_ref[...] = v.T
        vb_ref[...] = v.astype(BF16)
    u_ref[...] = z[:, 3 * ATTN_WIDTH:]


def _proj(x, gmix, win_bf, gq, gk, seg, tm, seq=None, key_block=None):
    n = x.shape[0]
    row = lambda w: pl.BlockSpec((tm, w), lambda i: (i, 0))
    full = lambda a: pl.BlockSpec(a.shape, lambda i: (0,) * a.ndim)
    rows_f32 = jax.ShapeDtypeStruct((n, ATTN_WIDTH), F32)
    rows_bf16 = jax.ShapeDtypeStruct((n, ATTN_WIDTH), BF16)
    if key_block is None:
        out_specs = [row(ATTN_WIDTH)] * 3 + [row(POOL_WIDTH)]
        out_shape = [rows_bf16, rows_f32, rows_f32, jax.ShapeDtypeStruct((n, POOL_WIDTH), F32)]
    else:
        batch, tiles = n // seq, seq // tm
        transposed = pl.BlockSpec((None, ATTN_WIDTH, tm), lambda i: (i // tiles, 0, i % tiles))
        t_shape = jax.ShapeDtypeStruct((batch, ATTN_WIDTH, seq), F32)
        blocks = tm // key_block
        out_specs = [row(ATTN_WIDTH), transposed,
                     pl.BlockSpec((None, blocks, ATTN_WIDTH, key_block), lambda i: (i // tiles, i % tiles, 0, 0)),
                     transposed, row(ATTN_WIDTH), row(POOL_WIDTH)]
        out_shape = [rows_bf16, t_shape,
                     jax.ShapeDtypeStruct((batch, seq // key_block, ATTN_WIDTH, key_block), BF16),
                     t_shape, rows_bf16, jax.ShapeDtypeStruct((n, POOL_WIDTH), F32)]
    return pl.pallas_call(
        functools.partial(_proj_kernel, key_block=key_block),
        grid=(n // tm,),
        in_specs=[row(D_MODEL), full(gmix), full(win_bf), full(gq), full(gk), full(seg)],
        out_specs=out_specs,
        out_shape=out_shape,
        compiler_params=_cparams("arbitrary"),
        name="proj",
    )(x, gmix, win_bf, gq, gk, seg)


SOFTPLUS_CLAMP = 64.0
ROW_CHUNK = 256


def _attn_kernel(bias_ref, q_ref, k_ref, v_ref, ntri_ref, o_ref, qm_ref, brow_ref, argp_ref, a_ref, totp_ref, c_ref,
                 acc_ref, *, tq, tk):
    hp = pl.program_id(1)
    i = pl.program_id(2)
    diag_blocks = tq // tk
    lane = lax.broadcasted_iota(jnp.int32, (tq, LANES), 1)
    q2 = q_ref[...]
    zero = jnp.zeros_like(q2)
    for h in range(2):
        mine = (lane < HEAD_DIM) if h == 0 else (lane >= HEAD_DIM)
        qm_ref[h, :, :LANES] = jnp.where(mine, q2, zero)
        qm_ref[h, :, LANES:] = jnp.where(lane // 2 == h, 1.0, 0.0).astype(BF16)
    brow = lax.broadcasted_iota(jnp.int32, (LANES, tk), 0)
    bias_rows = jnp.where(brow < 2, bias_ref[2 * hp], jnp.where(brow < 4, bias_ref[2 * hp + 1], 0.0))
    bias_hi = bias_rows.astype(BF16)
    bias_lo = (bias_rows - bias_hi.astype(F32)).astype(BF16)
    brow_ref[...] = jnp.where(brow % 2 == 0, bias_hi, bias_lo)
    chunks = [slice(r, r + ROW_CHUNK) for r in range(0, tq, ROW_CHUNK)]

    def causal(rs, d):
        row = lax.broadcasted_iota(jnp.int32, (ROW_CHUNK, tk), 0) + rs.start
        col = lax.broadcasted_iota(jnp.int32, (ROW_CHUNK, tk), 1) + d * tk
        return col < row

    def keys(kb):
        return jnp.concatenate([k_ref[kb], brow_ref[...]], axis=0)

    def values(kb):
        vs = v_ref[pl.ds(pl.multiple_of(kb * tk, tk), tk), :]
        vlane = lax.broadcasted_iota(jnp.int32, vs.shape, 1)
        vzero = jnp.zeros_like(vs)
        return jnp.concatenate([jnp.where(vlane < HEAD_DIM, vs, vzero), jnp.where(vlane >= HEAD_DIM, vs, vzero)],
                               axis=0)

    def scores(ks, rs):
        return [jnp.dot(qm_ref[h, rs], ks, preferred_element_type=F32) for h in range(2)]

    def suffix_sums(zs, rs, mask):
        for h, z in enumerate(zs):
            sp = jnp.maximum(jnp.log2(1.0 + jnp.exp2(jnp.minimum(z, SOFTPLUS_CLAMP))), z)
            spm = sp if mask is None else jnp.where(mask, sp, 0.0)
            later = jnp.dot(spm.astype(BF16), ntri_ref[...], preferred_element_type=F32)
            argp_ref[h, rs] = (z - sp) + later
            totp_ref[h, rs] = jnp.broadcast_to(jnp.sum(spm, axis=-1, keepdims=True), (ROW_CHUNK, LANES))

    def weights(rs, mask, first):
        for h in range(2):
            arg = argp_ref[h, rs]
            if first:
                c_ref[h, rs] = totp_ref[h, rs]
            else:
                c = c_ref[h, rs]
                arg = arg - jnp.concatenate([c] * (tk // LANES), axis=1)
                c_ref[h, rs] = c + totp_ref[h, rs]
            a = jnp.exp2(arg)
            if mask is not None:
                a = jnp.where(mask, a, 0.0)
            a_ref[rs, h * tk:(h + 1) * tk] = a.astype(BF16)

    def weigh_values(vs, rs, first):
        pv = jnp.dot(a_ref[rs, :], vs, preferred_element_type=F32)
        if first:
            acc_ref[rs, :] = pv
        else:
            acc_ref[rs, :] += pv

    assert diag_blocks == 2 and len(chunks) == 2
    lower, upper = chunks
    kb0 = i * diag_blocks
    ks0, ks1, vs0, vs1 = keys(kb0), keys(kb0 + 1), values(kb0), values(kb0 + 1)
    tri_mask = causal(lower, 0)
    z_up1, z_lo0, z_up0 = scores(ks1, upper), scores(ks0, lower), scores(ks0, upper)
    suffix_sums(z_up1, upper, tri_mask)
    suffix_sums(z_lo0, lower, tri_mask)
    weights(upper, tri_mask, first=True)
    suffix_sums(z_up0, upper, None)
    weigh_values(vs1, upper, first=True)
    weights(lower, tri_mask, first=True)
    weigh_values(vs0, lower, first=True)
    weights(upper, None, first=False)
    weigh_values(vs0, upper, first=False)

    n_full = i * diag_blocks

    @pl.when(n_full > 0)
    def _():
        ks0 = keys(n_full - 1)
        for rs in chunks:
            suffix_sums(scores(ks0, rs), rs, None)

        def body(j, carry):
            cur = n_full - 1 - j
            ks, vs = keys(cur - 1), values(cur)
            zs = []
            for rs in chunks:
                weights(rs, None, first=False)
                weigh_values(vs, rs, first=False)
                zs.append(scores(ks, rs))
            for rs, z in zip(chunks, zs):
                suffix_sums(z, rs, None)
            return carry

        lax.fori_loop(0, n_full - 1, body, 0)
        vs0 = values(0)
        for rs in chunks:
            weights(rs, None, first=False)
            weigh_values(vs0, rs, first=False)

    o_ref[...] = acc_ref[...].astype(BF16)


def _attention(bias2, q, ktb, vb, batch, seq, tq, tk):
    n = q.shape[0]
    nq = seq // tq
    nk = seq // tk
    ntri = jnp.asarray(-np.tril(np.ones((tk, tk), np.float32), -1), BF16)
    grid_spec = pltpu.PrefetchScalarGridSpec(
        num_scalar_prefetch=1,
        grid=(batch, N_HEADS // 2, nq),
        in_specs=[
            pl.BlockSpec((tq, LANES), lambda b, hp, i, bias: (b * nq + i, hp)),
            pl.BlockSpec((None, nk, LANES, tk), lambda b, hp, i, bias: (b, 0, hp, 0)),
            pl.BlockSpec((seq, LANES), lambda b, hp, i, bias: (b, hp)),
            pl.BlockSpec(ntri.shape, lambda b, hp, i, bias: (0, 0)),
        ],
        out_specs=pl.BlockSpec((tq, LANES), lambda b, hp, i, bias: (b * nq + i, hp)),
        scratch_shapes=[pltpu.VMEM((2, tq, 2 * LANES), BF16), pltpu.VMEM((LANES, tk), BF16),
                        pltpu.VMEM((2, tq, tk), F32), pltpu.VMEM((tq, 2 * tk), BF16),
                        pltpu.VMEM((2, tq, LANES), F32), pltpu.VMEM((2, tq, LANES), F32),
                        pltpu.VMEM((tq, LANES), F32)],
    )
    return pl.pallas_call(
        functools.partial(_attn_kernel, tq=tq, tk=tk),
        grid_spec=grid_spec,
        out_shape=jax.ShapeDtypeStruct((n, ATTN_WIDTH), BF16),
        compiler_params=_cparams("arbitrary", "arbitrary", "arbitrary"),
        name="attn_prompt",
    )(bias2, q, ktb, vb, ntri)


def _decode_kernel(pt_ref, q_ref, bias_ref, tri_ref, *rest, pages_per_step):
    k_refs = rest[:pages_per_step]
    v_refs = rest[pages_per_step:2 * pages_per_step]
    o_ref, c_ref, acc_ref = rest[2 * pages_per_step:]
    step = pl.program_id(1)

    @pl.when(step == 0)
    def _():
        c_ref[...] = jnp.zeros_like(c_ref)
        acc_ref[...] = jnp.zeros_like(acc_ref)

    head_row = lax.broadcasted_iota(jnp.int32, (N_HEADS, PAGE_SIZE), 0)
    zs = []
    for j in range(pages_per_step):
        zj = jnp.zeros((N_HEADS, PAGE_SIZE), F32)
        for h in range(N_HEADS):
            dims = slice(h * HEAD_DIM, (h + 1) * HEAD_DIM)
            zrow = jnp.sum(q_ref[dims, :] * k_refs[j][h], axis=0, keepdims=True)
            zj = jnp.where(head_row == h, zrow, zj)
        zs.append(zj)
    z = jnp.concatenate(zs, axis=0) + bias_ref[...]
    lk = -_softplus2(z)
    hi = lk.astype(BF16)
    lo = (lk - hi.astype(F32)).astype(BF16)
    r = (jnp.dot(hi, tri_ref[...], preferred_element_type=F32)
         + jnp.dot(lo, tri_ref[...], preferred_element_type=F32))
    tot = r[:, PAGE_SIZE:]
    c = c_ref[...]
    cs = []
    for j in range(pages_per_step):
        cs.append(c)
        c = c + tot[j * N_HEADS:(j + 1) * N_HEADS]
    c_ref[...] = c
    a = jnp.exp2(z + lk + r[:, :PAGE_SIZE] + jnp.concatenate(cs, axis=0))

    for h in range(N_HEADS):
        dims = slice(h * HEAD_DIM, (h + 1) * HEAD_DIM)
        acc = acc_ref[dims, :]
        for j in range(pages_per_step):
            row = j * N_HEADS + h
            acc = acc + a[row:row + 1, :] * v_refs[j][h]
        acc_ref[dims, :] = acc

    @pl.when(step == pl.num_programs(1) - 1)
    def _():
        o_ref[...] = jnp.sum(acc_ref[...], axis=1, keepdims=True)


def _decode_attention(page_table, q_lanes, bias_rows, cache_kt, cache_vt, pages_per_step):
    dec_b, n_pages = page_table.shape
    n_steps = n_pages // pages_per_step
    tri = np.zeros((PAGE_SIZE, 2 * PAGE_SIZE), np.float32)
    tri[:, :PAGE_SIZE] = np.tril(np.ones((PAGE_SIZE, PAGE_SIZE), np.float32), -1)
    tri[:, PAGE_SIZE:] = 1.0
    tri = jnp.asarray(tri, BF16)

    def page_spec(j):
        def index(b, s, pt):
            page = n_pages - 1 - (s * pages_per_step + j)
            return (pt[b * n_pages + page], 0, 0, 0)
        return pl.BlockSpec((None, N_HEADS, HEAD_DIM, PAGE_SIZE), index)

    const = lambda a: pl.BlockSpec(a.shape, lambda b, s, pt: (0,) * a.ndim)
    grid_spec = pltpu.PrefetchScalarGridSpec(
        num_scalar_prefetch=1,
        grid=(dec_b, n_steps),
        in_specs=[pl.BlockSpec((None, ATTN_WIDTH, PAGE_SIZE), lambda b, s, pt: (b, 0, 0)),
                  const(bias_rows), const(tri)]
                 + [page_spec(j) for j in range(pages_per_step)] * 2,
        out_specs=pl.BlockSpec((None, ATTN_WIDTH, 1), lambda b, s, pt: (b, 0, 0)),
        scratch_shapes=[pltpu.VMEM((N_HEADS, PAGE_SIZE), F32), pltpu.VMEM((ATTN_WIDTH, PAGE_SIZE), F32)],
    )
    out = pl.pallas_call(
        functools.partial(_decode_kernel, pages_per_step=pages_per_step),
        grid_spec=grid_spec,
        out_shape=jax.ShapeDtypeStruct((dec_b, ATTN_WIDTH, 1), F32),
        compiler_params=_cparams("arbitrary", "arbitrary"),
        name="attn_decode",
    )(page_table.reshape(-1), q_lanes, bias_rows, tri,
      *([cache_kt] * pages_per_step), *([cache_vt] * pages_per_step))
    return out.reshape(dec_b, ATTN_WIDTH)


def _post_common(x, attn_bf, pooled, wpool_ref, spool_ref, wout_ref, gffn_ref, wr_ref, br_ref, tri_ref,
                 cnt_sc, h_ref, m_ref, route_ref):
    tm = x.shape[0]
    ys = [jnp.dot(pooled[:, g * POOL_GROUP_WIDTH:(g + 1) * POOL_GROUP_WIDTH].astype(BF16), wpool_ref[g],
                  preferred_element_type=F32) for g in range(len(POOL_WINDOWS))]
    pool = (jnp.concatenate(ys, axis=1) * spool_ref[...]).astype(BF16)
    mixed = jnp.concatenate([attn_bf, pool], axis=1)
    h = x + jnp.dot(mixed, wout_ref[...], preferred_element_type=F32)
    h_ref[...] = h
    m = _rms(h, gffn_ref[...])
    m_ref[...] = m

    m_hi = m.astype(BF16)
    m_lo = (m - m_hi.astype(F32)).astype(BF16)
    wr = wr_ref[...]
    r_hi = jnp.dot(m_hi, wr, preferred_element_type=F32)
    r_lo = jnp.dot(m_lo, wr[:, :LANES], preferred_element_type=F32)
    logits = r_hi[:, :LANES] + r_hi[:, LANES:] + r_lo + br_ref[...]
    lane = lax.broadcasted_iota(jnp.int32, (tm, LANES), 1)
    lanef = lane.astype(F32)
    neg = -jnp.inf
    big = 1e9

    def first_argmax(vals):
        top = jnp.max(vals, axis=-1, keepdims=True)
        idx = jnp.min(jnp.where(vals == top, lanef, big), axis=-1, keepdims=True)
        return top, idx

    gmask = lane < N_EXPERT_GROUPS
    gmax, gidx = first_argmax(jnp.where(gmask, logits, neg))
    gsum = jnp.sum(jnp.where(gmask, jnp.exp(jnp.where(gmask, logits - gmax, 0.0)), 0.0), axis=-1, keepdims=True)
    g_w = 1.0 / gsum
    lo = N_EXPERT_GROUPS + EXPERTS_PER_GROUP * gidx
    le = jnp.where((lanef >= lo) & (lanef < lo + EXPERTS_PER_GROUP), logits, neg)
    m1, i1 = first_argmax(le)
    m2, i2 = first_argmax(jnp.where(lanef == i1, neg, le))
    e2 = jnp.exp(m2 - m1)
    den = 1.0 + e2
    gate1 = g_w / den
    gate2 = g_w * e2 / den
    ex1 = i1 - N_EXPERT_GROUPS
    ex2 = i2 - N_EXPERT_GROUPS

    oh1 = lanef == ex1
    oh2 = lanef == ex2
    onehot = jnp.where(oh1, 1.0, 0.0) + jnp.where(oh2, 1.0, 0.0)
    before = jnp.dot(tri_ref[...], onehot.astype(BF16), preferred_element_type=F32) + cnt_sc[...]
    rank1 = jnp.sum(jnp.where(oh1, before, 0.0), axis=-1, keepdims=True)
    rank2 = jnp.sum(jnp.where(oh2, before, 0.0), axis=-1, keepdims=True)
    cnt_sc[...] += jnp.sum(onehot, axis=0, keepdims=True)

    route = jnp.zeros((tm, LANES), F32)
    for col, val in enumerate((ex1, ex2, rank1, rank2, gate1, gate2)):
        route = jnp.where(lane == col, val, route)
    route_ref[...] = route


def _post_prompt_kernel(x_ref, attn_ref, u_ref, uprev_ref, cnt_in_ref,
                        wpool_ref, spool_ref, wout_ref, gffn_ref, wr_ref, br_ref, tri_ref,
                        h_ref, m_ref, route_ref, cnt_ref, ext_ref, cnt_sc, *, tiles_per_seq):
    i = pl.program_id(0)
    tm = x_ref.shape[0]

    @pl.when(i == 0)
    def _():
        cnt_sc[...] = cnt_in_ref[...]

    seq_tile = i % tiles_per_seq
    u = u_ref[...]
    ext_ref[:HALO, :] = jnp.where(seq_tile == 0, 0.0, uprev_ref[...])
    ext_ref[HALO:, :] = u
    pos1 = seq_tile * tm + lax.broadcasted_iota(jnp.int32, (tm, POOL_GROUP_WIDTH), 0) + 1
    parts = []
    for g, w in enumerate(POOL_WINDOWS):
        sl = slice(g * POOL_GROUP_WIDTH, (g + 1) * POOL_GROUP_WIDTH)
        ws = u[:, sl]
        for j in range(1, w):
            ws = ws + ext_ref[HALO - j:HALO - j + tm, sl]
        cnt = jnp.minimum(pos1, w).astype(F32)
        parts.append(ws / cnt - u[:, sl])
    pooled = jnp.concatenate(parts, axis=1)
    _post_common(x_ref[...], attn_ref[...], pooled, wpool_ref, spool_ref, wout_ref, gffn_ref, wr_ref, br_ref,
                 tri_ref, cnt_sc, h_ref, m_ref, route_ref)
    cnt_ref[...] = cnt_sc[...]


def _post_sample_kernel(x_ref, attn_ref, ext3_ref, cnt_in_ref,
                        wpool_ref, spool_ref, wout_ref, gffn_ref, wr_ref, br_ref, tri_ref,
                        h_ref, m_ref, route_ref, cnt_ref, cnt_sc, *, past_len):
    cnt_sc[...] = cnt_in_ref[...]
    n_rows = ext3_ref.shape[1]
    parts = []
    for g, w in enumerate(POOL_WINDOWS):
        sl = slice(g * POOL_GROUP_WIDTH, (g + 1) * POOL_GROUP_WIDTH)
        win = ext3_ref[:, n_rows - w:, sl]
        parts.append(jnp.sum(win, axis=1) / float(min(past_len + 1, w)) - ext3_ref[:, n_rows - 1, sl])
    pooled = jnp.concatenate(parts, axis=1)
    _post_common(x_ref[...], attn_ref[...], pooled, wpool_ref, spool_ref, wout_ref, gffn_ref, wr_ref, br_ref,
                 tri_ref, cnt_sc, h_ref, m_ref, route_ref)
    cnt_ref[...] = cnt_sc[...]


def _post_weights_specs(weights, nargs):
    return [pl.BlockSpec(a.shape, (lambda *_, nd=a.ndim: (0,) * nd)) for a in weights]


def _strict_lower(tm):
    return jnp.asarray(np.tril(np.ones((tm, tm), np.float32), -1), BF16)


def _post_out(n, tm):
    row = lambda w: pl.BlockSpec((tm, w), lambda i: (i, 0))
    specs = [row(D_MODEL), row(D_MODEL), row(LANES), pl.BlockSpec((1, LANES), lambda i: (0, 0))]
    shapes = [jax.ShapeDtypeStruct((n, D_MODEL), F32), jax.ShapeDtypeStruct((n, D_MODEL), F32),
              jax.ShapeDtypeStruct((n, LANES), F32), jax.ShapeDtypeStruct((1, LANES), F32)]
    return specs, shapes


def _post_prompt(x, attn, u, cnt_in, weights, seq, tm):
    n = x.shape[0]
    row = lambda w: pl.BlockSpec((tm, w), lambda i: (i, 0))
    halo_blocks = tm // HALO
    tri = _strict_lower(tm)
    out_specs, out_shape = _post_out(n, tm)
    return pl.pallas_call(
        functools.partial(_post_prompt_kernel, tiles_per_seq=seq // tm),
        grid=(n // tm,),
        in_specs=[row(D_MODEL), row(ATTN_WIDTH), row(POOL_WIDTH),
                  pl.BlockSpec((HALO, POOL_WIDTH), lambda i: (jnp.maximum(i * halo_blocks - 1, 0), 0)),
                  pl.BlockSpec((1, LANES), lambda i: (0, 0))]
                 + _post_weights_specs(weights + (tri,), 1),
        out_specs=out_specs,
        out_shape=out_shape,
        scratch_shapes=[pltpu.VMEM((tm + HALO, POOL_WIDTH), F32), pltpu.VMEM((1, LANES), F32)],
        compiler_params=_cparams("arbitrary"),
        name="post_prompt",
    )(x, attn, u, u, cnt_in, *weights, tri)


def _post_sample(x, attn, ext3, cnt_in, weights, past_len):
    n = x.shape[0]
    tri = _strict_lower(n)
    out_specs, out_shape = _post_out(n, n)
    full = lambda a: pl.BlockSpec(a.shape, lambda i, nd=a.ndim: (0,) * nd)
    return pl.pallas_call(
        functools.partial(_post_sample_kernel, past_len=past_len),
        grid=(1,),
        in_specs=[full(x), full(attn), full(ext3), full(cnt_in)] + _post_weights_specs(weights + (tri,), 1),
        out_specs=out_specs,
        out_shape=out_shape,
        scratch_shapes=[pltpu.VMEM((1, LANES), F32)],
        compiler_params=_cparams("arbitrary"),
        name="post_sample",
    )(x, attn, ext3, cnt_in, *weights, tri)


DMA_WAIT_ROWS = 256
ISSUE_UNROLL = 8


def _row_copy(src, src_row, dst, dst_row, sem):
    return pltpu.make_async_copy(src.at[pl.ds(src_row, 1)], dst.at[pl.ds(dst_row, 1)], sem)


def _wait_rows(src, dst, sem, n_rows):
    assert n_rows % DMA_WAIT_ROWS == 0 or n_rows < DMA_WAIT_ROWS
    chunk = min(n_rows, DMA_WAIT_ROWS)
    for _ in range(n_rows // chunk):
        pltpu.make_async_copy(src.at[pl.ds(0, chunk)], dst.at[pl.ds(0, chunk)], sem).wait()


def _dispatch_kernel(dest_ref, m_ref, xs_in_ref, xs_ref, sem, *, tokens_per_step):
    del xs_in_ref

    def body(g, carry):
        for u in range(ISSUE_UNROLL):
            t = g * ISSUE_UNROLL + u
            for k in range(2):
                _row_copy(m_ref, t, xs_ref, dest_ref[2 * t + k], sem).start(priority=k)
        return carry

    lax.fori_loop(0, tokens_per_step // ISSUE_UNROLL, body, 0)
    _wait_rows(m_ref, xs_ref, sem, 2 * tokens_per_step)


def _dispatch(dest, m, xs, tokens_per_step):
    n = m.shape[0]
    dest_block = dest.shape[0] // (n // tokens_per_step)
    return pl.pallas_call(
        functools.partial(_dispatch_kernel, tokens_per_step=tokens_per_step),
        grid=(n // tokens_per_step,),
        in_specs=[pl.BlockSpec((dest_block,), lambda i: (i,), memory_space=pltpu.SMEM),
                  pl.BlockSpec((tokens_per_step, D_MODEL), lambda i: (i, 0)),
                  pl.BlockSpec(memory_space=pl.ANY)],
        out_specs=pl.BlockSpec(memory_space=pl.ANY),
        out_shape=jax.ShapeDtypeStruct(xs.shape, xs.dtype),
        scratch_shapes=[pltpu.SemaphoreType.DMA(())],
        input_output_aliases={2: 0},
        compiler_params=_cparams("arbitrary"),
        name="moe_dispatch",
    )(dest, m, xs)


def _experts_kernel(blk_e_ref, blk_first_ref, n_used_ref, xs_ref, wg_ref, wu_ref, wd_ref, ys_ref,
                    wg_bf, wu_bf, wd_bf):
    i = pl.program_id(0)

    @pl.when(i < n_used_ref[0])
    def _():
        @pl.when(blk_first_ref[i] == 1)
        def _():
            wg_bf[...] = wg_ref[...].astype(BF16)
            wu_bf[...] = wu_ref[...].astype(BF16)
            wd_bf[...] = wd_ref[...].astype(BF16)

        x = xs_ref[...].astype(BF16)
        gate = jnp.dot(x, wg_bf[...], preferred_element_type=F32)
        up = jnp.dot(x, wu_bf[...], preferred_element_type=F32)
        hid = gate * jax.nn.sigmoid(gate) * up
        ys_ref[...] = jnp.dot(hid.astype(BF16), wd_bf[...], preferred_element_type=F32)

    @pl.when(i >= n_used_ref[0])
    def _():
        ys_ref[...] = jnp.zeros_like(ys_ref)


def _experts(blk_e, blk_first, n_used, xs, w_gate, w_up, w_down):
    n_blk = xs.shape[0] // MOE_BLOCK
    rows = lambda i, be, bf, nu: (jnp.maximum(jnp.minimum(i, nu[0] - 1), 0), 0)
    out_rows = lambda i, be, bf, nu: (i, 0)
    wsel = lambda i, be, bf, nu: (be[i], 0, 0)
    grid_spec = pltpu.PrefetchScalarGridSpec(
        num_scalar_prefetch=3,
        grid=(n_blk,),
        in_specs=[pl.BlockSpec((MOE_BLOCK, D_MODEL), rows),
                  pl.BlockSpec((None, D_MODEL, EXPERT_HIDDEN), wsel),
                  pl.BlockSpec((None, D_MODEL, EXPERT_HIDDEN), wsel),
                  pl.BlockSpec((None, EXPERT_HIDDEN, D_MODEL), wsel)],
        out_specs=pl.BlockSpec((MOE_BLOCK, D_MODEL), out_rows),
        scratch_shapes=[pltpu.VMEM((D_MODEL, EXPERT_HIDDEN), BF16), pltpu.VMEM((D_MODEL, EXPERT_HIDDEN), BF16),
                        pltpu.VMEM((EXPERT_HIDDEN, D_MODEL), BF16)],
    )
    return pl.pallas_call(
        _experts_kernel,
        grid_spec=grid_spec,
        out_shape=jax.ShapeDtypeStruct(xs.shape, F32),
        compiler_params=_cparams("arbitrary"),
        name="moe_experts",
    )(blk_e, blk_first, n_used, xs, w_gate, w_up, w_down)


def _final_kernel(dest_ref, h_ref, route_ref, p_ref, ys_ref, gple_ref, wpg_ref, wpp_ref, o_ref,
                  ya_ref, yb_ref, sem):
    tm = h_ref.shape[0]

    def body(g, carry):
        for u in range(ISSUE_UNROLL):
            t = g * ISSUE_UNROLL + u
            _row_copy(ys_ref, dest_ref[2 * t], ya_ref, t, sem).start(priority=0)
            _row_copy(ys_ref, dest_ref[2 * t + 1], yb_ref, t, sem).start(priority=1)
        return carry

    lax.fori_loop(0, tm // ISSUE_UNROLL, body, 0)
    _wait_rows(ys_ref, ya_ref, sem, tm)
    _wait_rows(ys_ref, yb_ref, sem, tm)

    route = route_ref[...]
    lane = lax.broadcasted_iota(jnp.int32, route.shape, 1)
    g1 = jnp.sum(jnp.where(lane == 4, route, 0.0), axis=-1, keepdims=True)
    g2 = jnp.sum(jnp.where(lane == 5, route, 0.0), axis=-1, keepdims=True)
    h = h_ref[...] + (g1 * ya_ref[...] + g2 * yb_ref[...])
    n = _rms(h, gple_ref[...]).astype(BF16)
    gate = jax.nn.sigmoid(jnp.dot(n, wpg_ref[...], preferred_element_type=F32))
    proj = jnp.dot(p_ref[...].astype(BF16), wpp_ref[...], preferred_element_type=F32)
    o_ref[...] = h + gate * proj


def _final(dest, h, route, p, ys, gple, wpg_bf, wpp_bf, tm):
    n = h.shape[0]
    dest_block = dest.shape[0] // (n // tm)
    row = lambda w: pl.BlockSpec((tm, w), lambda i: (i, 0))
    full = lambda a: pl.BlockSpec(a.shape, lambda i, nd=a.ndim: (0,) * nd)
    return pl.pallas_call(
        _final_kernel,
        grid=(n // tm,),
        in_specs=[pl.BlockSpec((dest_block,), lambda i: (i,), memory_space=pltpu.SMEM),
                  row(D_MODEL), row(LANES), row(PLE_DIM),
                  pl.BlockSpec(memory_space=pl.ANY),
                  full(gple), full(wpg_bf), full(wpp_bf)],
        out_specs=row(D_MODEL),
        out_shape=jax.ShapeDtypeStruct((n, D_MODEL), F32),
        scratch_shapes=[pltpu.VMEM((tm, D_MODEL), F32), pltpu.VMEM((tm, D_MODEL), F32),
                        pltpu.SemaphoreType.DMA(())],
        compiler_params=_cparams("arbitrary"),
        name="moe_combine_ple",
    )(dest, h, route, p, ys, gple, wpg_bf, wpp_bf)


SMEM_BLOCK = 1024


def _pad_to(a, size):
    return jnp.concatenate([a, jnp.zeros((size - a.shape[0],), a.dtype)])


def kernel(x_prompt, x_sample, cache_k, cache_v, state_pool, page_table, p_prompt, p_sample, g_mix, w_in, g_q, g_k, b_sb, w_pool, s_pool, w_out, g_ffn, w_router_group, b_router_group, w_router_expert, b_router_expert, w_gate, w_up, w_down, g_ple, w_ple_gate, w_ple_proj):
    batch, seq, _ = x_prompt.shape
    dec_b, dec_t, _ = x_sample.shape
    assert dec_t == 1 and g_mix.shape[0] == 1
    n_pages = page_table.shape[1]
    past_len = n_pages * PAGE_SIZE
    n_p = batch * seq
    n_s = dec_b

    row = lambda a: a.reshape(1, -1).astype(F32)
    gmix = row(g_mix[0])
    win_bf = w_in[0].astype(BF16)
    gq = row(jnp.tile(g_q[0], N_HEADS) * (SB_SCALE * LOG2E))
    gk = row(jnp.tile(g_k[0], N_HEADS))
    bias2 = (b_sb[0] * LOG2E).astype(F32)
    seg = jnp.asarray(np.kron(np.eye(N_HEADS, dtype=np.float32), np.ones((HEAD_DIM, HEAD_DIM), np.float32)), BF16)
    w_r = jnp.zeros((D_MODEL, LANES), F32)
    w_r = w_r.at[:, :N_EXPERT_GROUPS].set(w_router_group[0])
    w_r = w_r.at[:, N_EXPERT_GROUPS:N_EXPERT_GROUPS + N_EXPERTS].set(w_router_expert[0])
    b_r = jnp.zeros((1, LANES), F32)
    b_r = b_r.at[0, :N_EXPERT_GROUPS].set(b_router_group[0])
    b_r = b_r.at[0, N_EXPERT_GROUPS:N_EXPERT_GROUPS + N_EXPERTS].set(b_router_expert[0])
    w_r_hi = w_r.astype(BF16)
    w_r_split = jnp.concatenate([w_r_hi, (w_r - w_r_hi.astype(F32)).astype(BF16)], axis=1)
    post_w = (w_pool[0].astype(BF16), row(s_pool[0]), w_out[0].astype(BF16), row(g_ffn[0]), w_r_split, b_r)
    gple = row(g_ple[0])
    wpg_bf = w_ple_gate[0].astype(BF16)
    wpp_bf = w_ple_proj[0].astype(BF16)

    xp = x_prompt.reshape(n_p, D_MODEL)
    attn_block = 256
    q_p, kt_p, ktb_p, vt_p, vb_p, u_p = _proj(xp, gmix, win_bf, gq, gk, seg, tm=512, seq=seq, key_block=attn_block)
    attn_p = _attention(bias2, q_p, ktb_p, vb_p, batch, seq, tq=2 * attn_block, tk=attn_block)
    cnt0 = jnp.zeros((1, LANES), F32)
    h_p, m_p, route_p, cnt_p = _post_prompt(xp, attn_p, u_p, cnt0, post_w, seq, tm=256)

    xs_ = x_sample.reshape(n_s, D_MODEL)
    q_s, k_s, v_s, u_s = _proj(xs_, gmix, win_bf, gq, gk, seg, tm=n_s)
    pages_per_step = 16
    q_lanes = jnp.broadcast_to(q_s.astype(F32)[:, :, None], (n_s, ATTN_WIDTH, PAGE_SIZE))
    bias_rows = jnp.broadcast_to(jnp.tile(bias2, pages_per_step)[:, None], (pages_per_step * N_HEADS, PAGE_SIZE))
    attn_s = _decode_attention(page_table, q_lanes, bias_rows,
                               jnp.transpose(cache_k[0], (0, 2, 3, 1)), jnp.transpose(cache_v[0], (0, 2, 3, 1)),
                               pages_per_step=pages_per_step)
    zero_row = jnp.zeros((n_s, HALO - POOL_STATE - 1, POOL_WIDTH), F32)
    ext3 = jnp.concatenate([zero_row, state_pool[0], u_s[:, None, :]], axis=1)
    h_s, m_s, route_s, cnt = _post_sample(xs_, attn_s.astype(BF16), ext3, cnt_p, post_w, past_len)

    counts = cnt[0, :N_EXPERTS].astype(jnp.int32)
    nblk_e = (counts + MOE_BLOCK - 1) // MOE_BLOCK
    bends = jnp.cumsum(nblk_e)
    bstarts = bends - nblk_e
    pstarts = bstarts * MOE_BLOCK
    n_assign = 2 * (n_p + n_s)
    n_blk = n_assign // MOE_BLOCK + N_EXPERTS + (1 if n_assign % MOE_BLOCK else 0)
    n_used = bends[-1:]
    blk = jnp.arange(n_blk, dtype=jnp.int32)
    last_e = jnp.max(jnp.where(nblk_e > 0, jnp.arange(N_EXPERTS, dtype=jnp.int32), 0))
    blk_e = jnp.minimum(jnp.sum(blk[:, None] >= bends[None, :], axis=1).astype(jnp.int32), last_e)
    blk_first = jnp.any((blk[:, None] == bstarts[None, :]) & (nblk_e[None, :] > 0), axis=1).astype(jnp.int32)

    def dests(route):
        e = route[:, 0:2].astype(jnp.int32)
        rank = route[:, 2:4].astype(jnp.int32)
        start = jnp.sum(jnp.where(e[:, :, None] == jnp.arange(N_EXPERTS)[None, None, :], pstarts[None, None, :], 0),
                        axis=-1)
        return (start + rank).reshape(-1)

    dest_p = dests(route_p)
    dest_s = _pad_to(dests(route_s), SMEM_BLOCK)

    xs = jnp.zeros((n_blk * MOE_BLOCK, D_MODEL), F32)
    xs = _dispatch(dest_p, m_p, xs, tokens_per_step=SMEM_BLOCK // 2)
    xs = _dispatch(dest_s, m_s, xs, tokens_per_step=n_s)
    ys = _experts(blk_e, blk_first, n_used.astype(jnp.int32), xs, w_gate[0], w_up[0], w_down[0])

    y_p = _final(dest_p, h_p, route_p, p_prompt[0].reshape(n_p, PLE_DIM), ys, gple, wpg_bf, wpp_bf,
                 tm=SMEM_BLOCK // 2)
    y_s = _final(dest_s, h_s, route_s, p_sample[0].reshape(n_s, PLE_DIM), ys, gple, wpg_bf, wpp_bf, tm=n_s)

    heads = lambda a, b: a.reshape(1, b, -1, N_HEADS, HEAD_DIM)
    heads_t = lambda a: jnp.transpose(a.reshape(batch, N_HEADS, HEAD_DIM, seq), (0, 3, 1, 2))[None]
    pool_prompt = u_p.reshape(batch, seq, POOL_WIDTH)[:, seq - POOL_STATE:][None]
    pool_sample = ext3[:, HALO - POOL_STATE:][None]
    return (y_p.reshape(batch, seq, D_MODEL), y_s.reshape(dec_b, dec_t, D_MODEL),
            heads_t(kt_p), heads_t(vt_p), pool_prompt,
            heads(k_s, dec_b), heads(v_s, dec_b), pool_sample)
```
